```python
import math
import jax, jax.numpy as jnp
from jax import lax
import numpy as np

D_MODEL = 1024
BATCH = 16
SEQ = 2048
DEPTH = 1

D_FF = 2816
CHUNK = 128
D_A = 512
SGU_GROUPS = 4
SGU_GROUP_DIM = D_A // SGU_GROUPS
HEAD_DIM = 64
N_Q_HEADS = 8
N_KV_HEADS = 2
Q_PER_KV = N_Q_HEADS // N_KV_HEADS
WINDOW = 128
BLOCK = 128
D_B = N_Q_HEADS * HEAD_DIM
SPLITS = (D_A, D_A, D_B, N_KV_HEADS * HEAD_DIM, N_KV_HEADS * HEAD_DIM, D_MODEL, D_MODEL)
IN_COLS = sum(SPLITS)
N_MOD = 9
EPS = 1e-6
NEG = -1e30

kernel_name = "hybrid_gmlp_swa_sink_macaron_adaln"


def rms_norm(x, g):
    xf = x.astype(jnp.float32)
    y = xf * lax.rsqrt(jnp.mean(xf * xf, axis=-1, keepdims=True) + EPS)
    return (y * g.astype(jnp.float32)).astype(x.dtype)


def layer_norm(x, g, b):
    xf = x.astype(jnp.float32)
    mu = jnp.mean(xf, axis=-1, keepdims=True)
    var = jnp.mean(jnp.square(xf - mu), axis=-1, keepdims=True)
    y = (xf - mu) * lax.rsqrt(var + EPS)
    return (y * g.astype(jnp.float32) + b.astype(jnp.float32)).astype(x.dtype)


def modulate(xn, shift, scale):
    return xn * (1 + scale[:, None, :]) + shift[:, None, :]


def swiglu(x, w_gate, w_up, w_down):
    return (jax.nn.silu(x @ w_gate) * (x @ w_up)) @ w_down


def gmlp_sgu(u, v, g_ln, b_ln, w_s, b_s):
    bsz, seq, _ = v.shape
    n_chunks = seq // CHUNK
    v = layer_norm(v, g_ln, b_ln)
    vg = v.reshape(bsz, n_chunks, CHUNK, SGU_GROUPS, SGU_GROUP_DIM)
    causal = jnp.tril(jnp.ones((CHUNK, CHUNK), dtype=bool))
    ws = jnp.where(causal[None], w_s, jnp.zeros_like(w_s))
    z = jnp.einsum('gts,bnsgc->bntgc', ws, vg)
    z = z + b_s.T[None, None, :, :, None]
    return u * z.reshape(bsz, seq, D_A)


def sliding_window_attention(q, k, v, g_q, g_k, sinks):
    bsz, seq = q.shape[0], q.shape[1]
    nb = seq // BLOCK
    q = rms_norm(q, g_q)
    k = rms_norm(k, g_k)
    pad = ((0, 0), (BLOCK, 0), (0, 0), (0, 0))
    k_prev = jnp.pad(k, pad)[:, :seq]
    v_prev = jnp.pad(v, pad)[:, :seq]
    kb = jnp.concatenate([k_prev.reshape(bsz, nb, BLOCK, N_KV_HEADS, HEAD_DIM),
                          k.reshape(bsz, nb, BLOCK, N_KV_HEADS, HEAD_DIM)], axis=2)
    vb = jnp.concatenate([v_prev.reshape(bsz, nb, BLOCK, N_KV_HEADS, HEAD_DIM),
                          v.reshape(bsz, nb, BLOCK, N_KV_HEADS, HEAD_DIM)], axis=2)
    qb = q.reshape(bsz, nb, BLOCK, N_KV_HEADS, Q_PER_KV, HEAD_DIM)
    scores = jnp.einsum('bnqhgd,bnkhd->bnhgqk', qb, kb).astype(jnp.float32)
    scores = scores * (HEAD_DIM ** -0.5)
    blk = jnp.arange(nb)[:, None, None]
    qpos = blk * BLOCK + jnp.arange(BLOCK)[None, :, None]
    kpos = (blk - 1) * BLOCK + jnp.arange(2 * BLOCK)[None, None, :]
    diff = qpos - kpos
    valid = (diff >= 0) & (diff < WINDOW) & (kpos >= 0)
    scores = jnp.where(valid[None, :, None, None], scores, NEG)
    sink = jnp.broadcast_to(
        sinks.astype(jnp.float32).reshape(1, 1, N_KV_HEADS, Q_PER_KV, 1, 1),
        scores.shape[:-1] + (1,))
    probs = jax.nn.softmax(jnp.concatenate([scores, sink], axis=-1), axis=-1)[..., :-1]
    out = jnp.einsum('bnhgqk,bnkhd->bnqhgd', probs.astype(v.dtype), vb)
    return out.reshape(bsz, seq, D_B)


def setup_inputs(seed: int = 0) -> dict:
    key = jax.random.key(seed)
    ks = jax.random.split(key, 26)
    f32 = jnp.float32
    L = DEPTH

    def nrm(k, shape, fan_in):
        return jax.random.normal(k, shape, f32) * (fan_in ** -0.5)

    def gain(k, shape):
        return 1.0 + 0.02 * jax.random.normal(k, shape, f32)

    return {
        "x": jax.random.normal(ks[0], (BATCH, SEQ, D_MODEL), f32),
        "c": jax.random.normal(ks[1], (BATCH, D_MODEL), f32),
        "w_ada": nrm(ks[2], (L, D_MODEL, N_MOD * D_MODEL), D_MODEL) * 0.5,
        "b_ada": 0.01 * jax.random.normal(ks[3], (L, N_MOD * D_MODEL), f32),
        "g_norm1": gain(ks[4], (L, D_MODEL)),
        "ffn1_w_gate": nrm(ks[5], (L, D_MODEL, D_FF), D_MODEL),
        "ffn1_w_up": nrm(ks[6], (L, D_MODEL, D_FF), D_MODEL),
        "ffn1_w_down": nrm(ks[7], (L, D_FF, D_MODEL), D_FF),
        "g_norm2": gain(ks[8], (L, D_MODEL)),
        "w_in": nrm(ks[9], (L, D_MODEL, IN_COLS), D_MODEL),
        "g_sgu_ln": gain(ks[10], (L, D_A)),
        "b_sgu_ln": 0.01 * jax.random.normal(ks[11], (L, D_A), f32),
        "w_spatial": nrm(ks[12], (L, SGU_GROUPS, CHUNK, CHUNK), CHUNK),
        "b_spatial": 1.0 + 0.02 * jax.random.normal(ks[13], (L, SGU_GROUPS, CHUNK), f32),
        "g_q": gain(ks[14], (L, HEAD_DIM)),
        "g_k": gain(ks[15], (L, HEAD_DIM)),
        "attn_sinks": jax.random.normal(ks[16], (L, N_Q_HEADS), f32),
        "w_branch_a": nrm(ks[17], (L, D_A, D_MODEL), D_A),
        "w_branch_b": nrm(ks[18], (L, D_B, D_MODEL), D_B),
        "w_out": nrm(ks[19], (L, D_MODEL, D_MODEL), D_MODEL),
        "g_norm3": gain(ks[20], (L, D_MODEL)),
        "ffn2_w_gate": nrm(ks[21], (L, D_MODEL, D_FF), D_MODEL),
        "ffn2_w_up": nrm(ks[22], (L, D_MODEL, D_FF), D_MODEL),
        "ffn2_w_down": nrm(ks[23], (L, D_FF, D_MODEL), D_FF),
    }


def reference(x, c, w_ada, b_ada, g_norm1, ffn1_w_gate, ffn1_w_up, ffn1_w_down,
              g_norm2, w_in, g_sgu_ln, b_sgu_ln, w_spatial, b_spatial, g_q, g_k,
              attn_sinks, w_branch_a, w_branch_b, w_out, g_norm3,
              ffn2_w_gate, ffn2_w_up, ffn2_w_down):
    bsz, seq, _ = x.shape
    cond = jax.nn.silu(c)
    split_idx = list(np.cumsum(SPLITS)[:-1])
    h = x
    for l in range(DEPTH):
        mods = cond @ w_ada[l] + b_ada[l]
        sh1, sc1, ga1, sh2, sc2, ga2, sh3, sc3, ga3 = jnp.split(mods, N_MOD, axis=-1)

        xn = modulate(rms_norm(h, g_norm1[l]), sh1, sc1)
        h = h + 0.5 * ga1[:, None, :] * swiglu(xn, ffn1_w_gate[l], ffn1_w_up[l], ffn1_w_down[l])

        xn = modulate(rms_norm(h, g_norm2[l]), sh2, sc2)
        proj = xn @ w_in[l]
        a_u, a_v, q, k, v, gate_a, gate_b = jnp.split(proj, split_idx, axis=-1)

        y_a = gmlp_sgu(jax.nn.gelu(a_u), jax.nn.gelu(a_v), g_sgu_ln[l], b_sgu_ln[l],
                       w_spatial[l], b_spatial[l]) @ w_branch_a[l]

        y_b = sliding_window_attention(
            q.reshape(bsz, seq, N_Q_HEADS, HEAD_DIM),
            k.reshape(bsz, seq, N_KV_HEADS, HEAD_DIM),
            v.reshape(bsz, seq, N_KV_HEADS, HEAD_DIM),
            g_q[l], g_k[l], attn_sinks[l]) @ w_branch_b[l]

        merged = jax.nn.sigmoid(gate_a) * y_a + jax.nn.sigmoid(gate_b) * y_b
        h = h + ga2[:, None, :] * (merged @ w_out[l])

        xn = modulate(rms_norm(h, g_norm3[l]), sh3, sc3)
        h = h + 0.5 * ga3[:, None, :] * swiglu(xn, ffn2_w_gate[l], ffn2_w_up[l], ffn2_w_down[l])
    return h
```

```python
import functools

import jax
import jax.numpy as jnp
from jax import lax
from jax.experimental import pallas as pl
from jax.experimental.pallas import tpu as pltpu

F32 = jnp.float32
BF16 = jnp.bfloat16

D_MODEL = 1024
D_FF = 2816
CHUNK = 128
D_A = 512
SGU_GROUPS = 4
SGU_GROUP_DIM = D_A // SGU_GROUPS
HEAD_DIM = 64
N_Q_HEADS = 8
N_KV_HEADS = 2
D_B = N_Q_HEADS * HEAD_DIM
D_KV = N_KV_HEADS * HEAD_DIM
N_MOD = 9
EPS = 1e-6
NEG = -1e30

COL_UV = 0
COL_QKV = 2 * D_A
COL_GATES = COL_QKV + D_B + 2 * D_KV
IN_COLS = COL_GATES + 2 * D_MODEL

LANES = 128
MXU_DIM = 256
HEADS_PER_VREG = LANES // HEAD_DIM

TM_FFN = 512
TM_MIX = 512
FF_CHUNK = MXU_DIM
VMEM_LIMIT_FFN = 44 * 1024 * 1024
VMEM_LIMIT_MIX = 48 * 1024 * 1024


def _const_spec(shape):
    zeros = (0,) * len(shape)
    return pl.BlockSpec(shape, lambda i: zeros, pipeline_mode=pl.Buffered(1))


def _rms_modulate(h, g, mod_ref, row):
    ms = jnp.mean(h * h, axis=-1, keepdims=True)
    y = h * lax.rsqrt(ms + EPS) * g
    shift = mod_ref[0, row:row + 1, :]
    scale = mod_ref[0, row + 1:row + 2, :]
    return y * (1.0 + scale) + shift


def _mods_kernel(c_ref, w_ref, b_ref, o_ref):
    cond = jax.nn.silu(c_ref[...]).astype(BF16)
    o_ref[...] = jnp.dot(cond, w_ref[...].astype(BF16),
                         preferred_element_type=F32) + b_ref[...]


def _adaln_mods(c, w_ada, b_ada):
    bsz = c.shape[0]
    n_out = w_ada.shape[1]
    bn = D_MODEL
    return pl.pallas_call(
        _mods_kernel,
        out_shape=jax.ShapeDtypeStruct((bsz, n_out), F32),
        grid=(n_out // bn,),
        in_specs=[pl.BlockSpec((bsz, D_MODEL), lambda j: (0, 0)),
                  pl.BlockSpec((D_MODEL, bn), lambda j: (0, j)),
                  pl.BlockSpec((1, bn), lambda j: (0, j))],
        out_specs=pl.BlockSpec((bsz, bn), lambda j: (0, j)),
        compiler_params=pltpu.CompilerParams(dimension_semantics=("arbitrary",)),
        name="adaln_mods",
    )(c, w_ada, b_ada.reshape(1, n_out))


def _ffn_kernel(h_ref, mod_ref, gn_ref, wg_ref, wu_ref, wd_ref, o_ref,
                xn_ref, act_ref, *, mod_row):
    xn_ref[...] = _rms_modulate(h_ref[...], gn_ref[...], mod_ref, mod_row).astype(BF16)
    for c in range(D_FF // FF_CHUNK):
        cs = slice(c * FF_CHUNK, (c + 1) * FF_CHUNK)
        xn = xn_ref[...]
        g = jnp.dot(xn, wg_ref[:, cs], preferred_element_type=F32)
        u = jnp.dot(xn, wu_ref[:, cs], preferred_element_type=F32)
        act_ref[:, cs] = (jax.nn.silu(g) * u).astype(BF16)
    ff = jnp.dot(act_ref[...], wd_ref[...], preferred_element_type=F32)
    gate = mod_ref[0, mod_row + 2:mod_row + 3, :]
    o_ref[...] = h_ref[...] + 0.5 * gate * ff


def _ffn(h2d, mods3, g_norm, w_gate, w_up, w_down, *, mod_row, seq):
    n_tok = h2d.shape[0]
    tiles_per_seq = seq // TM_FFN
    row_spec = pl.BlockSpec((TM_FFN, D_MODEL), lambda i: (i, 0))
    return pl.pallas_call(
        functools.partial(_ffn_kernel, mod_row=mod_row),
        out_shape=jax.ShapeDtypeStruct((n_tok, D_MODEL), F32),
        grid=(n_tok // TM_FFN,),
        in_specs=[row_spec,
                  pl.BlockSpec((1, N_MOD, D_MODEL), lambda i: (i // tiles_per_seq, 0, 0)),
                  _const_spec((1, D_MODEL)),
                  _const_spec((D_MODEL, D_FF)),
                  _const_spec((D_MODEL, D_FF)),
                  _const_spec((D_FF, D_MODEL))],
        out_specs=row_spec,
        scratch_shapes=[pltpu.VMEM((TM_FFN, D_MODEL), BF16),
                        pltpu.VMEM((TM_FFN, D_FF), BF16)],
        compiler_params=pltpu.CompilerParams(
            dimension_semantics=("arbitrary",), vmem_limit_bytes=VMEM_LIMIT_FFN),
        name=f"ffn_mod{mod_row}",
    )(h2d, mods3, g_norm, w_gate, w_up, w_down)


def _head_rms(x, gain):
    lane = lax.broadcasted_iota(jnp.int32, (x.shape[0], LANES), 1)
    left = lane < HEAD_DIM
    cols = []
    for t in range(x.shape[1] // LANES):
        xt = x[:, t * LANES:(t + 1) * LANES]
        sq = xt * xt
        s_all = jnp.sum(sq, axis=-1, keepdims=True)
        s_left = jnp.sum(jnp.where(left, sq, 0.0), axis=-1, keepdims=True)
        ms = jnp.where(left, s_left, s_all - s_left) * (1.0 / HEAD_DIM)
        cols.append(xt * lax.rsqrt(ms + EPS))
    return jnp.concatenate(cols, axis=1) * gain


def _pair_blocks(x):
    lane = lax.broadcasted_iota(jnp.int32, x.shape, 1)
    left = lane < HEAD_DIM
    xr = pltpu.roll(x, HEAD_DIM, axis=1)
    blk0 = jnp.concatenate([jnp.where(left, x, 0.0), jnp.where(left, 0.0, xr)], axis=0)
    blk1 = jnp.concatenate([jnp.where(left, xr, 0.0), jnp.where(left, 0.0, x)], axis=0)
    return blk0.astype(BF16), blk1.astype(BF16)


def _mixer_kernel(sink_ref, h_ref, mod_ref, gn_ref, win_ref, gln_ref, bln_ref,
                  ws_ref, bs_ref, gq_ref, gk_ref, wa_ref, wb_ref, wo_ref, o_ref,
                  xn_ref, ya_ref, attn_ref, k_ref, v_ref, *, tiles_per_seq):
    tm = h_ref.shape[0]
    n_blk = tm // CHUNK
    first_tile = (pl.program_id(0) % tiles_per_seq) == 0

    xn_ref[...] = _rms_modulate(h_ref[...], gn_ref[...], mod_ref, 3).astype(BF16)

    uv = jnp.dot(xn_ref[...], win_ref[:, COL_UV:COL_UV + 2 * D_A], preferred_element_type=F32)
    u = jax.nn.gelu(uv[:, :D_A])
    v = jax.nn.gelu(uv[:, D_A:])
    mu = jnp.mean(v, axis=-1, keepdims=True)
    var = jnp.mean(jnp.square(v - mu), axis=-1, keepdims=True)
    vn = ((v - mu) * lax.rsqrt(var + EPS) * gln_ref[...] + bln_ref[...]).astype(BF16)
    t_idx = lax.broadcasted_iota(jnp.int32, (CHUNK, CHUNK), 0)
    s_idx = lax.broadcasted_iota(jnp.int32, (CHUNK, CHUNK), 1)
    causal = s_idx <= t_idx
    for g in range(SGU_GROUPS):
        gs = slice(g * SGU_GROUP_DIM, (g + 1) * SGU_GROUP_DIM)
        w_g = jnp.where(causal, ws_ref[g], 0.0).astype(BF16)
        for b in range(n_blk):
            rs = slice(b * CHUNK, (b + 1) * CHUNK)
            z = jnp.dot(w_g, vn[rs, gs], preferred_element_type=F32) + bs_ref[:, gs]
            ya_ref[rs, gs] = (u[rs, gs] * z).astype(BF16)
    y_a = jnp.dot(ya_ref[...], wa_ref[...], preferred_element_type=F32)

    qkv = jnp.dot(xn_ref[...], win_ref[:, COL_QKV:COL_QKV + D_B + 2 * D_KV],
                  preferred_element_type=F32)
    qn = _head_rms(qkv[:, :D_B], gq_ref[...]) * (HEAD_DIM ** -0.5)
    kn = _head_rms(qkv[:, D_B:D_B + D_KV], gk_ref[...])

    @pl.when(first_tile)
    def _():
        k_ref[0:CHUNK, :] = jnp.zeros((CHUNK, D_KV), F32)
        v_ref[0:CHUNK, :] = jnp.zeros((CHUNK, D_KV), F32)

    k_ref[CHUNK:, :] = kn
    v_ref[CHUNK:, :] = qkv[:, D_B + D_KV:]

    r_idx = lax.broadcasted_iota(jnp.int32, (CHUNK, 4 * CHUNK), 0)
    c_idx = lax.broadcasted_iota(jnp.int32, (CHUNK, 4 * CHUNK), 1) & (2 * CHUNK - 1)
    band = c_idx - r_idx - 1
    band_first = band - jnp.where(c_idx < CHUNK, jnp.where(first_tile, 8 * CHUNK, 0), 0)
    lane = lax.broadcasted_iota(jnp.int32, (CHUNK, LANES), 1)
    left = lane < HEAD_DIM
    for b in range(n_blk):
        rs = slice(b * CHUNK, (b + 1) * CHUNK)
        kblk = _pair_blocks(k_ref[b * CHUNK:(b + 2) * CHUNK, :])
        vblk = _pair_blocks(v_ref[b * CHUNK:(b + 2) * CHUNK, :])
        valid = lax.bitcast_convert_type(band_first if b == 0 else band,
                                         jnp.uint32) < jnp.uint32(CHUNK)
        for p in range(N_Q_HEADS // HEADS_PER_VREG):
            j = (p * HEADS_PER_VREG) // (N_Q_HEADS // N_KV_HEADS)
            q_pair = qn[rs, p * LANES:(p + 1) * LANES].astype(BF16)
            s = lax.dot_general(q_pair, kblk[j], (((1,), (1,)), ((), ())),
                                preferred_element_type=F32)
            s = jnp.where(valid, s, NEG)
            probs, inv = [], []
            for hh in range(HEADS_PER_VREG):
                sink = sink_ref[p * HEADS_PER_VREG + hh]
                sh = s[:, hh * 2 * CHUNK:(hh + 1) * 2 * CHUNK]
                m = jnp.maximum(jnp.max(sh, axis=-1, keepdims=True), sink)
                e = jnp.exp(sh - m)
                denom = jnp.sum(e, axis=-1, keepdims=True) + jnp.exp(sink - m)
                probs.append(e.astype(BF16))
                inv.append(1.0 / denom)
            pv = jnp.dot(jnp.concatenate(probs, axis=1), vblk[j], preferred_element_type=F32)
            attn_ref[rs, p * LANES:(p + 1) * LANES] = (
                pv * jnp.where(left, inv[0], inv[1])).astype(BF16)

    k_ref[0:CHUNK, :] = k_ref[tm:tm + CHUNK, :]
    v_ref[0:CHUNK, :] = v_ref[tm:tm + CHUNK, :]
    y_b = jnp.dot(attn_ref[...], wb_ref[...], preferred_element_type=F32)

    gates = jnp.dot(xn_ref[...], win_ref[:, COL_GATES:], preferred_element_type=F32)
    merged = (jax.nn.sigmoid(gates[:, :D_MODEL]) * y_a
              + jax.nn.sigmoid(gates[:, D_MODEL:]) * y_b).astype(BF16)
    out = jnp.dot(merged, wo_ref[...], preferred_element_type=F32)
    o_ref[...] = h_ref[...] + mod_ref[0, 5:6, :] * out


def _mixer(h2d, mods3, g_norm, w_in, g_ln, b_ln, w_spatial, bias_rows, g_q, g_k, sinks,
           w_a, w_b, w_out, *, seq):
    n_tok = h2d.shape[0]
    tiles_per_seq = seq // TM_MIX
    row_spec = pl.BlockSpec((TM_MIX, D_MODEL), lambda i: (i, 0))
    return pl.pallas_call(
        functools.partial(_mixer_kernel, tiles_per_seq=tiles_per_seq),
        out_shape=jax.ShapeDtypeStruct((n_tok, D_MODEL), F32),
        grid=(n_tok // TM_MIX,),
        in_specs=[pl.BlockSpec(memory_space=pltpu.SMEM),
                  row_spec,
                  pl.BlockSpec((1, N_MOD, D_MODEL), lambda i: (i // tiles_per_seq, 0, 0)),
                  _const_spec((1, D_MODEL)),
                  _const_spec((D_MODEL, IN_COLS)),
                  _const_spec((1, D_A)),
                  _const_spec((1, D_A)),
                  _const_spec((SGU_GROUPS, CHUNK, CHUNK)),
                  _const_spec((CHUNK, D_A)),
                  _const_spec((1, D_B)),
                  _const_spec((1, D_KV)),
                  _const_spec((D_A, D_MODEL)),
                  _const_spec((D_B, D_MODEL)),
                  _const_spec((D_MODEL, D_MODEL))],
        out_specs=row_spec,
        scratch_shapes=[pltpu.VMEM((TM_MIX, D_MODEL), BF16),
                        pltpu.VMEM((TM_MIX, D_A), BF16),
                        pltpu.VMEM((TM_MIX, D_B), BF16),
                        pltpu.VMEM((TM_MIX + CHUNK, D_KV), F32),
                        pltpu.VMEM((TM_MIX + CHUNK, D_KV), F32)],
        compiler_params=pltpu.CompilerParams(
            dimension_semantics=("arbitrary",), vmem_limit_bytes=VMEM_LIMIT_MIX),
        name="mixer",
    )(sinks, h2d, mods3, g_norm, w_in, g_ln, b_ln, w_spatial, bias_rows, g_q, g_k,
      w_a, w_b, w_out)


def kernel(x, c, w_ada, b_ada, g_norm1, ffn1_w_gate, ffn1_w_up, ffn1_w_down, g_norm2, w_in,
           g_sgu_ln, b_sgu_ln, w_spatial, b_spatial, g_q, g_k, attn_sinks, w_branch_a,
           w_branch_b, w_out, g_norm3, ffn2_w_gate, ffn2_w_up, ffn2_w_down):
    bsz, seq, d = x.shape
    assert d == D_MODEL and seq % TM_FFN == 0 and seq % TM_MIX == 0
    assert w_ada.shape[0] == 1, "single-layer problem"
    h = x.reshape(bsz * seq, d)
    mods3 = _adaln_mods(c, w_ada[0], b_ada[0]).reshape(bsz, N_MOD, d)

    h = _ffn(h, mods3, g_norm1, ffn1_w_gate[0].astype(BF16), ffn1_w_up[0].astype(BF16),
             ffn1_w_down[0].astype(BF16), mod_row=0, seq=seq)

    bias_rows = jnp.repeat(b_spatial[0].T, SGU_GROUP_DIM, axis=1)
    h = _mixer(h, mods3, g_norm2, w_in[0].astype(BF16), g_sgu_ln, b_sgu_ln, w_spatial[0],
               bias_rows, jnp.tile(g_q[0], N_Q_HEADS)[None], jnp.tile(g_k[0], N_KV_HEADS)[None],
               attn_sinks[0], w_branch_a[0].astype(BF16), w_branch_b[0].astype(BF16),
               w_out[0].astype(BF16), seq=seq)

    h = _ffn(h, mods3, g_norm3, ffn2_w_gate[0].astype(BF16), ffn2_w_up[0].astype(BF16),
             ffn2_w_down[0].astype(BF16), mod_row=6, seq=seq)
    return h.reshape(bsz, seq, d)
```

```python
import functools

import jax
import jax.numpy as jnp
from jax import lax
from jax.experimental import pallas as pl
from jax.experimental.pallas import tpu as pltpu

F32 = jnp.float32
BF16 = jnp.bfloat16

D_MODEL = 1024
D_FF = 2816
CHUNK = 128
D_A = 512
SGU_GROUPS = 4
SGU_GROUP_DIM = D_A // SGU_GROUPS
HEAD_DIM = 64
N_Q_HEADS = 8
N_KV_HEADS = 2
D_B = N_Q_HEADS * HEAD_DIM
D_KV = N_KV_HEADS * HEAD_DIM
N_MOD = 9
EPS = 1e-6
NEG = -1e30

COL_UV = 0
COL_QKV = 2 * D_A
COL_GATES = COL_QKV + D_B + 2 * D_KV
IN_COLS = COL_GATES + 2 * D_MODEL

LANES = 128
MXU_DIM = 256
HEADS_PER_VREG = LANES // HEAD_DIM

TM_FFN = 512
TM_MIX = 512
FF_CHUNK = MXU_DIM
VMEM_LIMIT_FFN = 44 * 1024 * 1024
VMEM_LIMIT_MIX = 48 * 1024 * 1024


def _const_spec(shape):
    zeros = (0,) * len(shape)
    return pl.BlockSpec(shape, lambda i: zeros, pipeline_mode=pl.Buffered(1))


def _rms_modulate(h, g, mod_ref, row):
    ms = jnp.mean(h * h, axis=-1, keepdims=True)
    y = h * lax.rsqrt(ms + EPS) * g
    shift = mod_ref[0, row:row + 1, :]
    scale = mod_ref[0, row + 1:row + 2, :]
    return y * (1.0 + scale) + shift


def _mods_kernel(c_ref, w_ref, b_ref, o_ref):
    cond = jax.nn.silu(c_ref[...]).astype(BF16)
    o_ref[...] = jnp.dot(cond, w_ref[...].astype(BF16),
                         preferred_element_type=F32) + b_ref[...]


def _adaln_mods(c, w_ada, b_ada):
    bsz = c.shape[0]
    n_out = w_ada.shape[1]
    bn = D_MODEL
    return pl.pallas_call(
        _mods_kernel,
        out_shape=jax.ShapeDtypeStruct((bsz, n_out), F32),
        grid=(n_out // bn,),
        in_specs=[pl.BlockSpec((bsz, D_MODEL), lambda j: (0, 0)),
                  pl.BlockSpec((D_MODEL, bn), lambda j: (0, j)),
                  pl.BlockSpec((1, bn), lambda j: (0, j))],
        out_specs=pl.BlockSpec((bsz, bn), lambda j: (0, j)),
        compiler_params=pltpu.CompilerParams(dimension_semantics=("arbitrary",)),
        name="adaln_mods",
    )(c, w_ada, b_ada.reshape(1, n_out))


def _ffn_kernel(h_ref, mod_ref, gn_ref, wg_ref, wu_ref, wd_ref, o_ref,
                xn_ref, act_ref, *, mod_row):
    xn_ref[...] = _rms_modulate(h_ref[...], gn_ref[...], mod_ref, mod_row).astype(BF16)
    for c in range(D_FF // FF_CHUNK):
        cs = slice(c * FF_CHUNK, (c + 1) * FF_CHUNK)
        xn = xn_ref[...]
        g = jnp.dot(xn, wg_ref[:, cs], preferred_element_type=F32)
        u = jnp.dot(xn, wu_ref[:, cs], preferred_element_type=F32)
        act_ref[:, cs] = (jax.nn.silu(g) * u).astype(BF16)
    ff = jnp.dot(act_ref[...], wd_ref[...], preferred_element_type=F32)
    gate = mod_ref[0, mod_row + 2:mod_row + 3, :]
    o_ref[...] = h_ref[...] + 0.5 * gate * ff


def _ffn(h2d, mods3, g_norm, w_gate, w_up, w_down, *, mod_row, seq):
    n_tok = h2d.shape[0]
    tiles_per_seq = seq // TM_FFN
    row_spec = pl.BlockSpec((TM_FFN, D_MODEL), lambda i: (i, 0))
    return pl.pallas_call(
        functools.partial(_ffn_kernel, mod_row=mod_row),
        out_shape=jax.ShapeDtypeStruct((n_tok, D_MODEL), F32),
        grid=(n_tok // TM_FFN,),
        in_specs=[row_spec,
                  pl.BlockSpec((1, N_MOD, D_MODEL), lambda i: (i // tiles_per_seq, 0, 0)),
                  _const_spec((1, D_MODEL)),
                  _const_spec((D_MODEL, D_FF)),
                  _const_spec((D_MODEL, D_FF)),
                  _const_spec((D_FF, D_MODEL))],
        out_specs=row_spec,
        scratch_shapes=[pltpu.VMEM((TM_FFN, D_MODEL), BF16),
                        pltpu.VMEM((TM_FFN, D_FF), BF16)],
        compiler_params=pltpu.CompilerParams(
            dimension_semantics=("arbitrary",), vmem_limit_bytes=VMEM_LIMIT_FFN),
        name=f"ffn_mod{mod_row}",
    )(h2d, mods3, g_norm, w_gate, w_up, w_down)


def _head_rms(x, gain):
    lane = lax.broadcasted_iota(jnp.int32, (x.shape[0], LANES), 1)
    left = lane < HEAD_DIM
    cols = []
    for t in range(x.shape[1] // LANES):
        xt = x[:, t * LANES:(t + 1) * LANES]
        sq = xt * xt
        s_all = jnp.sum(sq, axis=-1, keepdims=True)
        s_left = jnp.sum(jnp.where(left, sq, 0.0), axis=-1, keepdims=True)
        ms = jnp.where(left, s_left, s_all - s_left) * (1.0 / HEAD_DIM)
        cols.append(xt * lax.rsqrt(ms + EPS))
    return jnp.concatenate(cols, axis=1) * gain


def _pair_blocks(x):
    lane = lax.broadcasted_iota(jnp.int32, x.shape, 1)
    left = lane < HEAD_DIM
    xr = pltpu.roll(x, HEAD_DIM, axis=1)
    blk0 = jnp.concatenate([jnp.where(left, x, 0.0), jnp.where(left, 0.0, xr)], axis=0)
    blk1 = jnp.concatenate([jnp.where(left, xr, 0.0), jnp.where(left, 0.0, x)], axis=0)
    return blk0.astype(BF16), blk1.astype(BF16)


def _mixer_kernel(sink_ref, h_ref, mod_ref, gn_ref, win_ref, gln_ref, bln_ref,
                  ws_ref, bs_ref, gq_ref, gk_ref, wa_ref, wb_ref, wo_ref, o_ref,
                  xn_ref, uv_ref, vn_ref, ya_ref, qkv_ref, qn_ref, attn_ref, k_ref, v_ref,
                  sg_ref, mg_ref, *, tiles_per_seq):
    tm = h_ref.shape[0]
    n_blk = tm // CHUNK
    first_tile = (pl.program_id(0) % tiles_per_seq) == 0
    blocks = [slice(b * CHUNK, (b + 1) * CHUNK) for b in range(n_blk)]

    xn_ref[...] = _rms_modulate(h_ref[...], gn_ref[...], mod_ref, 3).astype(BF16)

    uv_ref[...] = jnp.dot(xn_ref[...], win_ref[:, COL_UV:COL_UV + 2 * D_A],
                          preferred_element_type=F32)
    for rs in blocks:
        v = jax.nn.gelu(uv_ref[rs, D_A:])
        mu = jnp.mean(v, axis=-1, keepdims=True)
        var = jnp.mean(jnp.square(v - mu), axis=-1, keepdims=True)
        vn_ref[rs, :] = ((v - mu) * lax.rsqrt(var + EPS) * gln_ref[...] + bln_ref[...]).astype(BF16)
        uv_ref[rs, :D_A] = jax.nn.gelu(uv_ref[rs, :D_A])
    t_idx = lax.broadcasted_iota(jnp.int32, (CHUNK, CHUNK), 0)
    s_idx = lax.broadcasted_iota(jnp.int32, (CHUNK, CHUNK), 1)
    causal = s_idx <= t_idx
    for g in range(SGU_GROUPS):
        gs = slice(g * SGU_GROUP_DIM, (g + 1) * SGU_GROUP_DIM)
        w_g = jnp.where(causal, ws_ref[g], 0.0).astype(BF16)
        v_cat = jnp.concatenate([vn_ref[rs, gs] for rs in blocks], axis=1)
        z = jnp.dot(w_g, v_cat, preferred_element_type=F32)
        for b, rs in enumerate(blocks):
            zb = z[:, b * SGU_GROUP_DIM:(b + 1) * SGU_GROUP_DIM] + bs_ref[:, gs]
            ya_ref[rs, gs] = (uv_ref[rs, gs] * zb).astype(BF16)

    qkv_ref[...] = jnp.dot(xn_ref[...], win_ref[:, COL_QKV:COL_QKV + D_B + 2 * D_KV],
                           preferred_element_type=F32)

    @pl.when(first_tile)
    def _():
        k_ref[0:CHUNK, :] = jnp.zeros((CHUNK, D_KV), F32)
        v_ref[0:CHUNK, :] = jnp.zeros((CHUNK, D_KV), F32)

    for b, rs in enumerate(blocks):
        qn_ref[rs, :] = (_head_rms(qkv_ref[rs, :D_B], gq_ref[...])
                         * (HEAD_DIM ** -0.5)).astype(BF16)
        k_ref[(b + 1) * CHUNK:(b + 2) * CHUNK, :] = _head_rms(
            qkv_ref[rs, D_B:D_B + D_KV], gk_ref[...])
        v_ref[(b + 1) * CHUNK:(b + 2) * CHUNK, :] = qkv_ref[rs, D_B + D_KV:]

    r_idx = lax.broadcasted_iota(jnp.int32, (CHUNK, 4 * CHUNK), 0)
    c_idx = lax.broadcasted_iota(jnp.int32, (CHUNK, 4 * CHUNK), 1) & (2 * CHUNK - 1)
    band = c_idx - r_idx - 1
    band_first = band - jnp.where(c_idx < CHUNK, jnp.where(first_tile, 8 * CHUNK, 0), 0)
    lane = lax.broadcasted_iota(jnp.int32, (CHUNK, LANES), 1)
    left = lane < HEAD_DIM
    e_head = lax.broadcasted_iota(jnp.int32, (4 * CHUNK, LANES), 0) // (2 * CHUNK)
    e_lane = lax.broadcasted_iota(jnp.int32, (4 * CHUNK, LANES), 1) // HEAD_DIM
    ones_cols = jnp.where(e_head == e_lane, 1.0, 0.0).astype(BF16)
    tiles_per_kv = (N_Q_HEADS // N_KV_HEADS) // HEADS_PER_VREG
    n_units = n_blk * N_KV_HEADS
    gate_w = 2 * D_MODEL // n_units
    unit = 0
    for b, rs in enumerate(blocks):
        kblk = _pair_blocks(k_ref[b * CHUNK:(b + 2) * CHUNK, :])
        vblk = _pair_blocks(v_ref[b * CHUNK:(b + 2) * CHUNK, :])
        valid = lax.bitcast_convert_type(band_first if b == 0 else band,
                                         jnp.uint32) < jnp.uint32(CHUNK)
        for j in range(N_KV_HEADS):
            tiles = [j * tiles_per_kv + t for t in range(tiles_per_kv)]
            q_rows = jnp.concatenate([qn_ref[rs, t * LANES:(t + 1) * LANES] for t in tiles], axis=0)
            s = lax.dot_general(q_rows, kblk[j], (((1,), (1,)), ((), ())),
                                preferred_element_type=F32)
            probs, sink_terms = [], []
            for i, t in enumerate(tiles):
                st = jnp.where(valid, s[i * CHUNK:(i + 1) * CHUNK, :], NEG)
                e_heads, e_sinks = [], []
                for hh in range(HEADS_PER_VREG):
                    sink = sink_ref[t * HEADS_PER_VREG + hh]
                    sh = st[:, hh * 2 * CHUNK:(hh + 1) * 2 * CHUNK]
                    m = jnp.maximum(jnp.max(sh, axis=-1, keepdims=True), sink)
                    e_heads.append(jnp.exp(sh - m).astype(BF16))
                    e_sinks.append(jnp.exp(sink - m))
                probs.append(jnp.concatenate(e_heads, axis=1))
                sink_terms.append(jnp.where(left, e_sinks[0], e_sinks[1]))
            pv = jnp.dot(jnp.concatenate(probs, axis=0),
                         jnp.concatenate([vblk[j], ones_cols], axis=1),
                         preferred_element_type=F32)
            out = pv[:, :LANES] / (pv[:, LANES:] + jnp.concatenate(sink_terms, axis=0))
            for i, t in enumerate(tiles):
                attn_ref[rs, t * LANES:(t + 1) * LANES] = out[i * CHUNK:(i + 1) * CHUNK].astype(BF16)
            gc = slice(unit * gate_w, (unit + 1) * gate_w)
            sg_ref[:, gc] = jax.nn.sigmoid(jnp.dot(
                xn_ref[...], win_ref[:, COL_GATES + unit * gate_w:COL_GATES + (unit + 1) * gate_w],
                preferred_element_type=F32))
            unit += 1

    k_ref[0:CHUNK, :] = k_ref[tm:tm + CHUNK, :]
    v_ref[0:CHUNK, :] = v_ref[tm:tm + CHUNK, :]

    for c in range(D_MODEL // MXU_DIM):
        cs = slice(c * MXU_DIM, (c + 1) * MXU_DIM)
        cs_b = slice(D_MODEL + c * MXU_DIM, D_MODEL + (c + 1) * MXU_DIM)
        y_a = jnp.dot(ya_ref[...], wa_ref[:, cs], preferred_element_type=F32)
        y_b = jnp.dot(attn_ref[...], wb_ref[:, cs], preferred_element_type=F32)
        mg_ref[:, cs] = (sg_ref[:, cs] * y_a + sg_ref[:, cs_b] * y_b).astype(BF16)
    out = jnp.dot(mg_ref[...], wo_ref[...], preferred_element_type=F32)
    o_ref[...] = h_ref[...] + mod_ref[0, 5:6, :] * out


def _mixer(h2d, mods3, g_norm, w_in, g_ln, b_ln, w_spatial, bias_rows, g_q, g_k, sinks,
           w_a, w_b, w_out, *, seq):
    n_tok = h2d.shape[0]
    tiles_per_seq = seq // TM_MIX
    row_spec = pl.BlockSpec((TM_MIX, D_MODEL), lambda i: (i, 0))
    return pl.pallas_call(
        functools.partial(_mixer_kernel, tiles_per_seq=tiles_per_seq),
        out_shape=jax.ShapeDtypeStruct((n_tok, D_MODEL), F32),
        grid=(n_tok // TM_MIX,),
        in_specs=[pl.BlockSpec(memory_space=pltpu.SMEM),
                  row_spec,
                  pl.BlockSpec((1, N_MOD, D_MODEL), lambda i: (i // tiles_per_seq, 0, 0)),
                  _const_spec((1, D_MODEL)),
                  _const_spec((D_MODEL, IN_COLS)),
                  _const_spec((1, D_A)),
                  _const_spec((1, D_A)),
                  _const_spec((SGU_GROUPS, CHUNK, CHUNK)),
                  _const_spec((CHUNK, D_A)),
                  _const_spec((1, D_B)),
                  _const_spec((1, D_KV)),
                  _const_spec((D_A, D_MODEL)),
                  _const_spec((D_B, D_MODEL)),
                  _const_spec((D_MODEL, D_MODEL))],
        out_specs=row_spec,
        scratch_shapes=[pltpu.VMEM((TM_MIX, D_MODEL), BF16),
                        pltpu.VMEM((TM_MIX, 2 * D_A), F32),
                        pltpu.VMEM((TM_MIX, D_A), BF16),
                        pltpu.VMEM((TM_MIX, D_A), BF16),
                        pltpu.VMEM((TM_MIX, D_B + 2 * D_KV), F32),
                        pltpu.VMEM((TM_MIX, D_B), BF16),
                        pltpu.VMEM((TM_MIX, D_B), BF16),
                        pltpu.VMEM((TM_MIX + CHUNK, D_KV), F32),
                        pltpu.VMEM((TM_MIX + CHUNK, D_KV), F32),
                        pltpu.VMEM((TM_MIX, 2 * D_MODEL), F32),
                        pltpu.VMEM((TM_MIX, D_MODEL), BF16)],
        compiler_params=pltpu.CompilerParams(
            dimension_semantics=("arbitrary",), vmem_limit_bytes=VMEM_LIMIT_MIX),
        name="mixer",
    )(sinks, h2d, mods3, g_norm, w_in, g_ln, b_ln, w_spatial, bias_rows, g_q, g_k,
      w_a, w_b, w_out)


def kernel(x, c, w_ada, b_ada, g_norm1, ffn1_w_gate, ffn1_w_up, ffn1_w_down, g_norm2, w_in,
           g_sgu_ln, b_sgu_ln, w_spatial, b_spatial, g_q, g_k, attn_sinks, w_branch_a,
           w_branch_b, w_out, g_norm3, ffn2_w_gate, ffn2_w_up, ffn2_w_down):
    bsz, seq, d = x.shape
    assert d == D_MODEL and seq % TM_FFN == 0 and seq % TM_MIX == 0
    assert w_ada.shape[0] == 1, "single-layer problem"
    h = x.reshape(bsz * seq, d)
    mods3 = _adaln_mods(c, w_ada[0], b_ada[0]).reshape(bsz, N_MOD, d)

    h = _ffn(h, mods3, g_norm1, ffn1_w_gate[0].astype(BF16), ffn1_w_up[0].astype(BF16),
             ffn1_w_down[0].astype(BF16), mod_row=0, seq=seq)

    bias_rows = jnp.repeat(b_spatial[0].T, SGU_GROUP_DIM, axis=1)
    h = _mixer(h, mods3, g_norm2, w_in[0].astype(BF16), g_sgu_ln, b_sgu_ln, w_spatial[0],
               bias_rows, jnp.tile(g_q[0], N_Q_HEADS)[None], jnp.tile(g_k[0], N_KV_HEADS)[None],
               attn_sinks[0], w_branch_a[0].astype(BF16), w_branch_b[0].astype(BF16),
               w_out[0].astype(BF16), seq=seq)

    h = _ffn(h, mods3, g_norm3, ffn2_w_gate[0].astype(BF16), ffn2_w_up[0].astype(BF16),
             ffn2_w_down[0].astype(BF16), mod_row=6, seq=seq)
    return h.reshape(bsz, seq, d)
```

```python
import functools

import jax
import jax.numpy as jnp
from jax import lax
from jax.experimental import pallas as pl
from jax.experimental.pallas import tpu as pltpu

F32 = jnp.float32
BF16 = jnp.bfloat16

D_MODEL = 1024
D_FF = 2816
CHUNK = 128
D_A = 512
SGU_GROUPS = 4
SGU_GROUP_DIM = D_A // SGU_GROUPS
HEAD_DIM = 64
N_Q_HEADS = 8
N_KV_HEADS = 2
D_B = N_Q_HEADS * HEAD_DIM
D_KV = N_KV_HEADS * HEAD_DIM
N_MOD = 9
EPS = 1e-6
NEG = -1e30

COL_UV = 0
COL_QKV = 2 * D_A
COL_GATES = COL_QKV + D_B + 2 * D_KV
IN_COLS = COL_GATES + 2 * D_MODEL

LANES = 128
MXU_DIM = 256
HEADS_PER_VREG = LANES // HEAD_DIM

TM_FFN = 512
TM_MIX = 512
FF_CHUNK = MXU_DIM
VMEM_LIMIT_FFN = 44 * 1024 * 1024
VMEM_LIMIT_MIX = 48 * 1024 * 1024


def _const_spec(shape):
    zeros = (0,) * len(shape)
    return pl.BlockSpec(shape, lambda i: zeros, pipeline_mode=pl.Buffered(1))


def _rms_modulate(h, g, mod_ref, row):
    ms = jnp.mean(h * h, axis=-1, keepdims=True)
    y = h * lax.rsqrt(ms + EPS) * g
    shift = mod_ref[0, row:row + 1, :]
    scale = mod_ref[0, row + 1:row + 2, :]
    return y * (1.0 + scale) + shift


def _mods_kernel(c_ref, w_ref, b_ref, o_ref):
    cond = jax.nn.silu(c_ref[...]).astype(BF16)
    o_ref[...] = jnp.dot(cond, w_ref[...].astype(BF16),
                         preferred_element_type=F32) + b_ref[...]


def _adaln_mods(c, w_ada, b_ada):
    bsz = c.shape[0]
    n_out = w_ada.shape[1]
    bn = D_MODEL
    return pl.pallas_call(
        _mods_kernel,
        out_shape=jax.ShapeDtypeStruct((bsz, n_out), F32),
        grid=(n_out // bn,),
        in_specs=[pl.BlockSpec((bsz, D_MODEL), lambda j: (0, 0)),
                  pl.BlockSpec((D_MODEL, bn), lambda j: (0, j)),
                  pl.BlockSpec((1, bn), lambda j: (0, j))],
        out_specs=pl.BlockSpec((bsz, bn), lambda j: (0, j)),
        compiler_params=pltpu.CompilerParams(dimension_semantics=("arbitrary",)),
        name="adaln_mods",
    )(c, w_ada, b_ada.reshape(1, n_out))


def _ffn_kernel(h_ref, mod_ref, gn_ref, wg_ref, wu_ref, wd_ref, o_ref,
                xn_ref, act_ref, *, mod_row):
    xn_ref[...] = _rms_modulate(h_ref[...], gn_ref[...], mod_ref, mod_row).astype(BF16)
    for c in range(D_FF // FF_CHUNK):
        cs = slice(c * FF_CHUNK, (c + 1) * FF_CHUNK)
        xn = xn_ref[...]
        g = jnp.dot(xn, wg_ref[:, cs], preferred_element_type=F32)
        u = jnp.dot(xn, wu_ref[:, cs], preferred_element_type=F32)
        act_ref[:, cs] = (jax.nn.silu(g) * u).astype(BF16)
    ff = jnp.dot(act_ref[...], wd_ref[...], preferred_element_type=F32)
    gate = mod_ref[0, mod_row + 2:mod_row + 3, :]
    o_ref[...] = h_ref[...] + 0.5 * gate * ff


def _ffn(h2d, mods3, g_norm, w_gate, w_up, w_down, *, mod_row, seq):
    n_tok = h2d.shape[0]
    tiles_per_seq = seq // TM_FFN
    row_spec = pl.BlockSpec((TM_FFN, D_MODEL), lambda i: (i, 0))
    return pl.pallas_call(
        functools.partial(_ffn_kernel, mod_row=mod_row),
        out_shape=jax.ShapeDtypeStruct((n_tok, D_MODEL), F32),
        grid=(n_tok // TM_FFN,),
        in_specs=[row_spec,
                  pl.BlockSpec((1, N_MOD, D_MODEL), lambda i: (i // tiles_per_seq, 0, 0)),
                  _const_spec((1, D_MODEL)),
                  _const_spec((D_MODEL, D_FF)),
                  _const_spec((D_MODEL, D_FF)),
                  _const_spec((D_FF, D_MODEL))],
        out_specs=row_spec,
        scratch_shapes=[pltpu.VMEM((TM_FFN, D_MODEL), BF16),
                        pltpu.VMEM((TM_FFN, D_FF), BF16)],
        compiler_params=pltpu.CompilerParams(
            dimension_semantics=("arbitrary",), vmem_limit_bytes=VMEM_LIMIT_FFN),
        name=f"ffn_mod{mod_row}",
    )(h2d, mods3, g_norm, w_gate, w_up, w_down)


def _head_rms(x, gain):
    lane = lax.broadcasted_iota(jnp.int32, (x.shape[0], LANES), 1)
    left = lane < HEAD_DIM
    cols = []
    for t in range(x.shape[1] // LANES):
        xt = x[:, t * LANES:(t + 1) * LANES]
        sq = xt * xt
        s_all = jnp.sum(sq, axis=-1, keepdims=True)
        s_left = jnp.sum(jnp.where(left, sq, 0.0), axis=-1, keepdims=True)
        ms = jnp.where(left, s_left, s_all - s_left) * (1.0 / HEAD_DIM)
        cols.append(xt * lax.rsqrt(ms + EPS))
    return jnp.concatenate(cols, axis=1) * gain


def _pair_blocks(x):
    lane = lax.broadcasted_iota(jnp.int32, x.shape, 1)
    left = lane < HEAD_DIM
    xr = pltpu.roll(x, HEAD_DIM, axis=1)
    blk0 = jnp.concatenate([jnp.where(left, x, 0.0), jnp.where(left, 0.0, xr)], axis=0)
    blk1 = jnp.concatenate([jnp.where(left, xr, 0.0), jnp.where(left, 0.0, x)], axis=0)
    return blk0.astype(BF16), blk1.astype(BF16)


def _mixer_kernel(sink_ref, h_ref, mod_ref, gn_ref, win_ref, gln_ref, bln_ref,
                  ws_ref, bs_ref, gq_ref, gk_ref, wa_ref, wb_ref, wo_ref, o_ref,
                  xn_ref, uv_ref, vn_ref, ya_ref, qkv_ref, qn_ref, attn_ref, k_ref, v_ref,
                  sg_ref, mg_ref, *, tiles_per_seq):
    tm = h_ref.shape[0]
    n_blk = tm // CHUNK
    first_tile = (pl.program_id(0) % tiles_per_seq) == 0
    blocks = [slice(b * CHUNK, (b + 1) * CHUNK) for b in range(n_blk)]

    @pl.when(first_tile)
    def _():
        k_ref[0:CHUNK, :] = jnp.zeros((CHUNK, D_KV), F32)
        v_ref[0:CHUNK, :] = jnp.zeros((CHUNK, D_KV), F32)

    xn_ref[...] = _rms_modulate(h_ref[...], gn_ref[...], mod_ref, 3).astype(BF16)

    n_gate_slices = 2 * n_blk
    gate_w = 2 * D_MODEL // n_gate_slices
    gate_slices = iter(range(n_gate_slices))

    def gate_slice():
        c0 = next(gate_slices) * gate_w
        sg_ref[:, c0:c0 + gate_w] = jax.nn.sigmoid(jnp.dot(
            xn_ref[...], win_ref[:, COL_GATES + c0:COL_GATES + c0 + gate_w],
            preferred_element_type=F32))

    uv_ref[...] = jnp.dot(xn_ref[...], win_ref[:, COL_UV:COL_UV + 2 * D_A],
                          preferred_element_type=F32)
    qkv_ref[...] = jnp.dot(xn_ref[...], win_ref[:, COL_QKV:COL_QKV + D_B + 2 * D_KV],
                           preferred_element_type=F32)

    for rs in blocks:
        v = jax.nn.gelu(uv_ref[rs, D_A:])
        mu = jnp.mean(v, axis=-1, keepdims=True)
        var = jnp.mean(jnp.square(v - mu), axis=-1, keepdims=True)
        vn_ref[rs, :] = ((v - mu) * lax.rsqrt(var + EPS) * gln_ref[...] + bln_ref[...]).astype(BF16)
        uv_ref[rs, :D_A] = jax.nn.gelu(uv_ref[rs, :D_A])
        gate_slice()
    t_idx = lax.broadcasted_iota(jnp.int32, (CHUNK, CHUNK), 0)
    s_idx = lax.broadcasted_iota(jnp.int32, (CHUNK, CHUNK), 1)
    causal = s_idx <= t_idx
    for g in range(SGU_GROUPS):
        gs = slice(g * SGU_GROUP_DIM, (g + 1) * SGU_GROUP_DIM)
        w_g = jnp.where(causal, ws_ref[g], 0.0).astype(BF16)
        v_cat = jnp.concatenate([vn_ref[rs, gs] for rs in blocks], axis=1)
        z = jnp.dot(w_g, v_cat, preferred_element_type=F32)
        for b, rs in enumerate(blocks):
            zb = z[:, b * SGU_GROUP_DIM:(b + 1) * SGU_GROUP_DIM] + bs_ref[:, gs]
            ya_ref[rs, gs] = (uv_ref[rs, gs] * zb).astype(BF16)

    for b, rs in enumerate(blocks):
        qn_ref[rs, :] = (_head_rms(qkv_ref[rs, :D_B], gq_ref[...])
                         * (HEAD_DIM ** -0.5)).astype(BF16)
        k_ref[(b + 1) * CHUNK:(b + 2) * CHUNK, :] = _head_rms(
            qkv_ref[rs, D_B:D_B + D_KV], gk_ref[...])
        v_ref[(b + 1) * CHUNK:(b + 2) * CHUNK, :] = qkv_ref[rs, D_B + D_KV:]

    r_idx = lax.broadcasted_iota(jnp.int32, (CHUNK, 4 * CHUNK), 0)
    c_idx = lax.broadcasted_iota(jnp.int32, (CHUNK, 4 * CHUNK), 1) & (2 * CHUNK - 1)
    band = c_idx - r_idx - 1
    band_first = band - jnp.where(c_idx < CHUNK, jnp.where(first_tile, 8 * CHUNK, 0), 0)
    lane = lax.broadcasted_iota(jnp.int32, (CHUNK, LANES), 1)
    left = lane < HEAD_DIM
    e_head = lax.broadcasted_iota(jnp.int32, (4 * CHUNK, LANES), 0) // (2 * CHUNK)
    e_lane = lax.broadcasted_iota(jnp.int32, (4 * CHUNK, LANES), 1) // HEAD_DIM
    ones_cols = jnp.where(e_head == e_lane, 1.0, 0.0).astype(BF16)
    tiles_per_kv = (N_Q_HEADS // N_KV_HEADS) // HEADS_PER_VREG
    for b, rs in enumerate(blocks):
        kblk = _pair_blocks(k_ref[b * CHUNK:(b + 2) * CHUNK, :])
        vblk = _pair_blocks(v_ref[b * CHUNK:(b + 2) * CHUNK, :])
        valid = lax.bitcast_convert_type(band_first if b == 0 else band,
                                         jnp.uint32) < jnp.uint32(CHUNK)
        for j in range(N_KV_HEADS):
            tiles = [j * tiles_per_kv + t for t in range(tiles_per_kv)]
            q_rows = jnp.concatenate([qn_ref[rs, t * LANES:(t + 1) * LANES] for t in tiles], axis=0)
            s = lax.dot_general(q_rows, kblk[j], (((1,), (1,)), ((), ())),
                                preferred_element_type=F32)
            probs, sink_terms = [], []
            for i, t in enumerate(tiles):
                st = jnp.where(valid, s[i * CHUNK:(i + 1) * CHUNK, :], NEG)
                e_heads, e_sinks = [], []
                for hh in range(HEADS_PER_VREG):
                    sink = sink_ref[t * HEADS_PER_VREG + hh]
                    sh = st[:, hh * 2 * CHUNK:(hh + 1) * 2 * CHUNK]
                    m = jnp.maximum(jnp.max(sh, axis=-1, keepdims=True), sink)
                    e_heads.append(jnp.exp(sh - m).astype(BF16))
                    e_sinks.append(jnp.exp(sink - m))
                probs.append(jnp.concatenate(e_heads, axis=1))
                sink_terms.append(jnp.where(left, e_sinks[0], e_sinks[1]))
            pv = jnp.dot(jnp.concatenate(probs, axis=0),
                         jnp.concatenate([vblk[j], ones_cols], axis=1),
                         preferred_element_type=F32)
            out = pv[:, :LANES] / (pv[:, LANES:] + jnp.concatenate(sink_terms, axis=0))
            for i, t in enumerate(tiles):
                attn_ref[rs, t * LANES:(t + 1) * LANES] = out[i * CHUNK:(i + 1) * CHUNK].astype(BF16)
        gate_slice()

    k_ref[0:CHUNK, :] = k_ref[tm:tm + CHUNK, :]
    v_ref[0:CHUNK, :] = v_ref[tm:tm + CHUNK, :]

    for c in range(D_MODEL // MXU_DIM):
        cs = slice(c * MXU_DIM, (c + 1) * MXU_DIM)
        cs_b = slice(D_MODEL + c * MXU_DIM, D_MODEL + (c + 1) * MXU_DIM)
        y_a = jnp.dot(ya_ref[...], wa_ref[:, cs], preferred_element_type=F32)
        y_b = jnp.dot(attn_ref[...], wb_ref[:, cs], preferred_element_type=F32)
        mg_ref[:, cs] = (sg_ref[:, cs] * y_a + sg_ref[:, cs_b] * y_b).astype(BF16)
    out = jnp.dot(mg_ref[...], wo_ref[...], preferred_element_type=F32)
    o_ref[...] = h_ref[...] + mod_ref[0, 5:6, :] * out


def _mixer(h2d, mods3, g_norm, w_in, g_ln, b_ln, w_spatial, bias_rows, g_q, g_k, sinks,
           w_a, w_b, w_out, *, seq):
    n_tok = h2d.shape[0]
    tiles_per_seq = seq // TM_MIX
    row_spec = pl.BlockSpec((TM_MIX, D_MODEL), lambda i: (i, 0))
    return pl.pallas_call(
        functools.partial(_mixer_kernel, tiles_per_seq=tiles_per_seq),
        out_shape=jax.ShapeDtypeStruct((n_tok, D_MODEL), F32),
        grid=(n_tok // TM_MIX,),
        in_specs=[pl.BlockSpec(memory_space=pltpu.SMEM),
                  row_spec,
                  pl.BlockSpec((1, N_MOD, D_MODEL), lambda i: (i // tiles_per_seq, 0, 0)),
                  _const_spec((1, D_MODEL)),
                  _const_spec((D_MODEL, IN_COLS)),
                  _const_spec((1, D_A)),
                  _const_spec((1, D_A)),
                  _const_spec((SGU_GROUPS, CHUNK, CHUNK)),
                  _const_spec((CHUNK, D_A)),
                  _const_spec((1, D_B)),
                  _const_spec((1, D_KV)),
                  _const_spec((D_A, D_MODEL)),
                  _const_spec((D_B, D_MODEL)),
                  _const_spec((D_MODEL, D_MODEL))],
        out_specs=row_spec,
        scratch_shapes=[pltpu.VMEM((TM_MIX, D_MODEL), BF16),
                        pltpu.VMEM((TM_MIX, 2 * D_A), F32),
                        pltpu.VMEM((TM_MIX, D_A), BF16),
                        pltpu.VMEM((TM_MIX, D_A), BF16),
                        pltpu.VMEM((TM_MIX, D_B + 2 * D_KV), F32),
                        pltpu.VMEM((TM_MIX, D_B), BF16),
                        pltpu.VMEM((TM_MIX, D_B), BF16),
                        pltpu.VMEM((TM_MIX + CHUNK, D_KV), F32),
                        pltpu.VMEM((TM_MIX + CHUNK, D_KV), F32),
                        pltpu.VMEM((TM_MIX, 2 * D_MODEL), F32),
                        pltpu.VMEM((TM_MIX, D_MODEL), BF16)],
        compiler_params=pltpu.CompilerParams(
            dimension_semantics=("arbitrary",), vmem_limit_bytes=VMEM_LIMIT_MIX),
        name="mixer",
    )(sinks, h2d, mods3, g_norm, w_in, g_ln, b_ln, w_spatial, bias_rows, g_q, g_k,
      w_a, w_b, w_out)


def kernel(x, c, w_ada, b_ada, g_norm1, ffn1_w_gate, ffn1_w_up, ffn1_w_down, g_norm2, w_in,
           g_sgu_ln, b_sgu_ln, w_spatial, b_spatial, g_q, g_k, attn_sinks, w_branch_a,
           w_branch_b, w_out, g_norm3, ffn2_w_gate, ffn2_w_up, ffn2_w_down):
    bsz, seq, d = x.shape
    assert d == D_MODEL and seq % TM_FFN == 0 and seq % TM_MIX == 0
    assert w_ada.shape[0] == 1, "single-layer problem"
    h = x.reshape(bsz * seq, d)
    mods3 = _adaln_mods(c, w_ada[0], b_ada[0]).reshape(bsz, N_MOD, d)

    h = _ffn(h, mods3, g_norm1, ffn1_w_gate[0].astype(BF16), ffn1_w_up[0].astype(BF16),
             ffn1_w_down[0].astype(BF16), mod_row=0, seq=seq)

    bias_rows = jnp.repeat(b_spatial[0].T, SGU_GROUP_DIM, axis=1)
    h = _mixer(h, mods3, g_norm2, w_in[0].astype(BF16), g_sgu_ln, b_sgu_ln, w_spatial[0],
               bias_rows, jnp.tile(g_q[0], N_Q_HEADS)[None], jnp.tile(g_k[0], N_KV_HEADS)[None],
               attn_sinks[0], w_branch_a[0].astype(BF16), w_branch_b[0].astype(BF16),
               w_out[0].astype(BF16), seq=seq)

    h = _ffn(h, mods3, g_norm3, ffn2_w_gate[0].astype(BF16), ffn2_w_up[0].astype(BF16),
             ffn2_w_down[0].astype(BF16), mod_row=6, seq=seq)
    return h.reshape(bsz, seq, d)
```

```python
import functools

import jax
import jax.numpy as jnp
from jax import lax
from jax.experimental import pallas as pl
from jax.experimental.pallas import tpu as pltpu

F32 = jnp.float32
BF16 = jnp.bfloat16

D_MODEL = 1024
D_FF = 2816
CHUNK = 128
D_A = 512
SGU_GROUPS = 4
SGU_GROUP_DIM = D_A // SGU_GROUPS
HEAD_DIM = 64
N_Q_HEADS = 8
N_KV_HEADS = 2
D_B = N_Q_HEADS * HEAD_DIM
D_KV = N_KV_HEADS * HEAD_DIM
N_MOD = 9
EPS = 1e-6
NEG = -1e30

COL_UV = 0
COL_QKV = 2 * D_A
COL_GATES = COL_QKV + D_B + 2 * D_KV
IN_COLS = COL_GATES + 2 * D_MODEL

LANES = 128
MXU_DIM = 256
HEADS_PER_VREG = LANES // HEAD_DIM

TM_FFN = 512
TM_MIX = 512
FF_CHUNK = MXU_DIM
VMEM_LIMIT_FFN = 58 * 1024 * 1024
VMEM_LIMIT_MIX = 48 * 1024 * 1024


def _const_spec(shape):
    zeros = (0,) * len(shape)
    return pl.BlockSpec(shape, lambda i: zeros, pipeline_mode=pl.Buffered(1))


def _rms_modulate(h, g, mod_ref, row):
    ms = jnp.mean(h * h, axis=-1, keepdims=True)
    y = h * lax.rsqrt(ms + EPS) * g
    shift = mod_ref[0, row:row + 1, :]
    scale = mod_ref[0, row + 1:row + 2, :]
    return y * (1.0 + scale) + shift


def _mods_kernel(c_ref, w_ref, b_ref, o_ref):
    cond = jax.nn.silu(c_ref[...]).astype(BF16)
    o_ref[...] = jnp.dot(cond, w_ref[...].astype(BF16),
                         preferred_element_type=F32) + b_ref[...]


def _adaln_mods(c, w_ada, b_ada):
    bsz = c.shape[0]
    n_out = w_ada.shape[1]
    bn = D_MODEL
    return pl.pallas_call(
        _mods_kernel,
        out_shape=jax.ShapeDtypeStruct((bsz, n_out), F32),
        grid=(n_out // bn,),
        in_specs=[pl.BlockSpec((bsz, D_MODEL), lambda j: (0, 0)),
                  pl.BlockSpec((D_MODEL, bn), lambda j: (0, j)),
                  pl.BlockSpec((1, bn), lambda j: (0, j))],
        out_specs=pl.BlockSpec((bsz, bn), lambda j: (0, j)),
        compiler_params=pltpu.CompilerParams(dimension_semantics=("arbitrary",)),
        name="adaln_mods",
    )(c, w_ada, b_ada.reshape(1, n_out))


def _ffn_kernel(h_ref, mod_ref, gn_ref, wg_ref, wu_ref, wd_ref, o_ref,
                xn_ref, act_ref, *, mod_row):
    xn_ref[...] = _rms_modulate(h_ref[...], gn_ref[...], mod_ref, mod_row)
    for c in range(D_FF // FF_CHUNK):
        cs = slice(c * FF_CHUNK, (c + 1) * FF_CHUNK)
        xn = xn_ref[...]
        g = jnp.dot(xn, wg_ref[:, cs], preferred_element_type=F32)
        u = jnp.dot(xn, wu_ref[:, cs], preferred_element_type=F32)
        act_ref[:, cs] = jax.nn.silu(g) * u
    ff = jnp.dot(act_ref[...], wd_ref[...], preferred_element_type=F32)
    gate = mod_ref[0, mod_row + 2:mod_row + 3, :]
    o_ref[...] = h_ref[...] + 0.5 * gate * ff


def _ffn(h2d, mods3, g_norm, w_gate, w_up, w_down, *, mod_row, seq):
    n_tok = h2d.shape[0]
    tiles_per_seq = seq // TM_FFN
    row_spec = pl.BlockSpec((TM_FFN, D_MODEL), lambda i: (i, 0))
    return pl.pallas_call(
        functools.partial(_ffn_kernel, mod_row=mod_row),
        out_shape=jax.ShapeDtypeStruct((n_tok, D_MODEL), F32),
        grid=(n_tok // TM_FFN,),
        in_specs=[row_spec,
                  pl.BlockSpec((1, N_MOD, D_MODEL), lambda i: (i // tiles_per_seq, 0, 0)),
                  _const_spec((1, D_MODEL)),
                  _const_spec((D_MODEL, D_FF)),
                  _const_spec((D_MODEL, D_FF)),
                  _const_spec((D_FF, D_MODEL))],
        out_specs=row_spec,
        scratch_shapes=[pltpu.VMEM((TM_FFN, D_MODEL), F32),
                        pltpu.VMEM((TM_FFN, D_FF), F32)],
        compiler_params=pltpu.CompilerParams(
            dimension_semantics=("arbitrary",), vmem_limit_bytes=VMEM_LIMIT_FFN),
        name=f"ffn_mod{mod_row}",
    )(h2d, mods3, g_norm, w_gate, w_up, w_down)


def _head_rms(x, gain):
    lane = lax.broadcasted_iota(jnp.int32, (x.shape[0], LANES), 1)
    left = lane < HEAD_DIM
    cols = []
    for t in range(x.shape[1] // LANES):
        xt = x[:, t * LANES:(t + 1) * LANES]
        sq = xt * xt
        s_all = jnp.sum(sq, axis=-1, keepdims=True)
        s_left = jnp.sum(jnp.where(left, sq, 0.0), axis=-1, keepdims=True)
        ms = jnp.where(left, s_left, s_all - s_left) * (1.0 / HEAD_DIM)
        cols.append(xt * lax.rsqrt(ms + EPS))
    return jnp.concatenate(cols, axis=1) * gain


def _pair_blocks(x):
    lane = lax.broadcasted_iota(jnp.int32, x.shape, 1)
    left = lane < HEAD_DIM
    xr = pltpu.roll(x, HEAD_DIM, axis=1)
    blk0 = jnp.concatenate([jnp.where(left, x, 0.0), jnp.where(left, 0.0, xr)], axis=0)
    blk1 = jnp.concatenate([jnp.where(left, xr, 0.0), jnp.where(left, 0.0, x)], axis=0)
    return blk0.astype(BF16), blk1.astype(BF16)


def _mixer_kernel(sink_ref, h_ref, mod_ref, gn_ref, win_ref, gln_ref, bln_ref,
                  ws_ref, bs_ref, gq_ref, gk_ref, wa_ref, wb_ref, wo_ref, o_ref,
                  xn_ref, uv_ref, vn_ref, ya_ref, qkv_ref, qn_ref, attn_ref, k_ref, v_ref,
                  sg_ref, mg_ref, *, tiles_per_seq):
    tm = h_ref.shape[0]
    n_blk = tm // CHUNK
    first_tile = (pl.program_id(0) % tiles_per_seq) == 0
    blocks = [slice(b * CHUNK, (b + 1) * CHUNK) for b in range(n_blk)]

    @pl.when(first_tile)
    def _():
        k_ref[0:CHUNK, :] = jnp.zeros((CHUNK, D_KV), F32)
        v_ref[0:CHUNK, :] = jnp.zeros((CHUNK, D_KV), F32)

    xn_ref[...] = _rms_modulate(h_ref[...], gn_ref[...], mod_ref, 3).astype(BF16)

    n_gate_slices = 2 * n_blk
    gate_w = 2 * D_MODEL // n_gate_slices
    gate_slices = iter(range(n_gate_slices))

    def gate_slice():
        c0 = next(gate_slices) * gate_w
        sg_ref[:, c0:c0 + gate_w] = jax.nn.sigmoid(jnp.dot(
            xn_ref[...], win_ref[:, COL_GATES + c0:COL_GATES + c0 + gate_w],
            preferred_element_type=F32))

    uv_ref[...] = jnp.dot(xn_ref[...], win_ref[:, COL_UV:COL_UV + 2 * D_A],
                          preferred_element_type=F32)
    qkv_ref[...] = jnp.dot(xn_ref[...], win_ref[:, COL_QKV:COL_QKV + D_B + 2 * D_KV],
                           preferred_element_type=F32)

    for rs in blocks:
        v = jax.nn.gelu(uv_ref[rs, D_A:])
        mu = jnp.mean(v, axis=-1, keepdims=True)
        var = jnp.mean(jnp.square(v - mu), axis=-1, keepdims=True)
        vn_ref[rs, :] = ((v - mu) * lax.rsqrt(var + EPS) * gln_ref[...] + bln_ref[...]).astype(BF16)
        uv_ref[rs, :D_A] = jax.nn.gelu(uv_ref[rs, :D_A])
        gate_slice()
    t_idx = lax.broadcasted_iota(jnp.int32, (CHUNK, CHUNK), 0)
    s_idx = lax.broadcasted_iota(jnp.int32, (CHUNK, CHUNK), 1)
    causal = s_idx <= t_idx
    for g in range(SGU_GROUPS):
        gs = slice(g * SGU_GROUP_DIM, (g + 1) * SGU_GROUP_DIM)
        w_g = jnp.where(causal, ws_ref[g], 0.0).astype(BF16)
        v_cat = jnp.concatenate([vn_ref[rs, gs] for rs in blocks], axis=1)
        z = jnp.dot(w_g, v_cat, preferred_element_type=F32)
        for b, rs in enumerate(blocks):
            zb = z[:, b * SGU_GROUP_DIM:(b + 1) * SGU_GROUP_DIM] + bs_ref[:, gs]
            ya_ref[rs, gs] = (uv_ref[rs, gs] * zb).astype(BF16)

    for b, rs in enumerate(blocks):
        qn_ref[rs, :] = (_head_rms(qkv_ref[rs, :D_B], gq_ref[...])
                         * (HEAD_DIM ** -0.5)).astype(BF16)
        k_ref[(b + 1) * CHUNK:(b + 2) * CHUNK, :] = _head_rms(
            qkv_ref[rs, D_B:D_B + D_KV], gk_ref[...])
        v_ref[(b + 1) * CHUNK:(b + 2) * CHUNK, :] = qkv_ref[rs, D_B + D_KV:]

    r_idx = lax.broadcasted_iota(jnp.int32, (CHUNK, 4 * CHUNK), 0)
    c_idx = lax.broadcasted_iota(jnp.int32, (CHUNK, 4 * CHUNK), 1) & (2 * CHUNK - 1)
    band = c_idx - r_idx - 1
    band_first = band - jnp.where(c_idx < CHUNK, jnp.where(first_tile, 8 * CHUNK, 0), 0)
    lane = lax.broadcasted_iota(jnp.int32, (CHUNK, LANES), 1)
    left = lane < HEAD_DIM
    e_head = lax.broadcasted_iota(jnp.int32, (4 * CHUNK, LANES), 0) // (2 * CHUNK)
    e_lane = lax.broadcasted_iota(jnp.int32, (4 * CHUNK, LANES), 1) // HEAD_DIM
    ones_cols = jnp.where(e_head == e_lane, 1.0, 0.0).astype(BF16)
    tiles_per_kv = (N_Q_HEADS // N_KV_HEADS) // HEADS_PER_VREG
    for b, rs in enumerate(blocks):
        kblk = _pair_blocks(k_ref[b * CHUNK:(b + 2) * CHUNK, :])
        vblk = _pair_blocks(v_ref[b * CHUNK:(b + 2) * CHUNK, :])
        valid = lax.bitcast_convert_type(band_first if b == 0 else band,
                                         jnp.uint32) < jnp.uint32(CHUNK)
        for j in range(N_KV_HEADS):
            tiles = [j * tiles_per_kv + t for t in range(tiles_per_kv)]
            q_rows = jnp.concatenate([qn_ref[rs, t * LANES:(t + 1) * LANES] for t in tiles], axis=0)
            s = lax.dot_general(q_rows, kblk[j], (((1,), (1,)), ((), ())),
                                preferred_element_type=F32)
            probs, sink_terms = [], []
            for i, t in enumerate(tiles):
                st = jnp.where(valid, s[i * CHUNK:(i + 1) * CHUNK, :], NEG)
                e_heads, e_sinks = [], []
                for hh in range(HEADS_PER_VREG):
                    sink = sink_ref[t * HEADS_PER_VREG + hh]
                    sh = st[:, hh * 2 * CHUNK:(hh + 1) * 2 * CHUNK]
                    m = jnp.maximum(jnp.max(sh, axis=-1, keepdims=True), sink)
                    e_heads.append(jnp.exp(sh - m).astype(BF16))
                    e_sinks.append(jnp.exp(sink - m))
                probs.append(jnp.concatenate(e_heads, axis=1))
                sink_terms.append(jnp.where(left, e_sinks[0], e_sinks[1]))
            pv = jnp.dot(jnp.concatenate(probs, axis=0),
                         jnp.concatenate([vblk[j], ones_cols], axis=1),
                         preferred_element_type=F32)
            out = pv[:, :LANES] / (pv[:, LANES:] + jnp.concatenate(sink_terms, axis=0))
            for i, t in enumerate(tiles):
                attn_ref[rs, t * LANES:(t + 1) * LANES] = out[i * CHUNK:(i + 1) * CHUNK].astype(BF16)
        gate_slice()

    k_ref[0:CHUNK, :] = k_ref[tm:tm + CHUNK, :]
    v_ref[0:CHUNK, :] = v_ref[tm:tm + CHUNK, :]

    for c in range(D_MODEL // MXU_DIM):
        cs = slice(c * MXU_DIM, (c + 1) * MXU_DIM)
        cs_b = slice(D_MODEL + c * MXU_DIM, D_MODEL + (c + 1) * MXU_DIM)
        y_a = jnp.dot(ya_ref[...], wa_ref[:, cs], preferred_element_type=F32)
        y_b = jnp.dot(attn_ref[...], wb_ref[:, cs], preferred_element_type=F32)
        mg_ref[:, cs] = (sg_ref[:, cs] * y_a + sg_ref[:, cs_b] * y_b).astype(BF16)
    out = jnp.dot(mg_ref[...], wo_ref[...], preferred_element_type=F32)
    o_ref[...] = h_ref[...] + mod_ref[0, 5:6, :] * out


def _mixer(h2d, mods3, g_norm, w_in, g_ln, b_ln, w_spatial, bias_rows, g_q, g_k, sinks,
           w_a, w_b, w_out, *, seq):
    n_tok = h2d.shape[0]
    tiles_per_seq = seq // TM_MIX
    row_spec = pl.BlockSpec((TM_MIX, D_MODEL), lambda i: (i, 0))
    return pl.pallas_call(
        functools.partial(_mixer_kernel, tiles_per_seq=tiles_per_seq),
        out_shape=jax.ShapeDtypeStruct((n_tok, D_MODEL), F32),
        grid=(n_tok // TM_MIX,),
        in_specs=[pl.BlockSpec(memory_space=pltpu.SMEM),
                  row_spec,
                  pl.BlockSpec((1, N_MOD, D_MODEL), lambda i: (i // tiles_per_seq, 0, 0)),
                  _const_spec((1, D_MODEL)),
                  _const_spec((D_MODEL, IN_COLS)),
                  _const_spec((1, D_A)),
                  _const_spec((1, D_A)),
                  _const_spec((SGU_GROUPS, CHUNK, CHUNK)),
                  _const_spec((CHUNK, D_A)),
                  _const_spec((1, D_B)),
                  _const_spec((1, D_KV)),
                  _const_spec((D_A, D_MODEL)),
                  _const_spec((D_B, D_MODEL)),
                  _const_spec((D_MODEL, D_MODEL))],
        out_specs=row_spec,
        scratch_shapes=[pltpu.VMEM((TM_MIX, D_MODEL), BF16),
                        pltpu.VMEM((TM_MIX, 2 * D_A), F32),
                        pltpu.VMEM((TM_MIX, D_A), BF16),
                        pltpu.VMEM((TM_MIX, D_A), BF16),
                        pltpu.VMEM((TM_MIX, D_B + 2 * D_KV), F32),
                        pltpu.VMEM((TM_MIX, D_B), BF16),
                        pltpu.VMEM((TM_MIX, D_B), BF16),
                        pltpu.VMEM((TM_MIX + CHUNK, D_KV), F32),
                        pltpu.VMEM((TM_MIX + CHUNK, D_KV), F32),
                        pltpu.VMEM((TM_MIX, 2 * D_MODEL), F32),
                        pltpu.VMEM((TM_MIX, D_MODEL), BF16)],
        compiler_params=pltpu.CompilerParams(
            dimension_semantics=("arbitrary",), vmem_limit_bytes=VMEM_LIMIT_MIX),
        name="mixer",
    )(sinks, h2d, mods3, g_norm, w_in, g_ln, b_ln, w_spatial, bias_rows, g_q, g_k,
      w_a, w_b, w_out)


def kernel(x, c, w_ada, b_ada, g_norm1, ffn1_w_gate, ffn1_w_up, ffn1_w_down, g_norm2, w_in,
           g_sgu_ln, b_sgu_ln, w_spatial, b_spatial, g_q, g_k, attn_sinks, w_branch_a,
           w_branch_b, w_out, g_norm3, ffn2_w_gate, ffn2_w_up, ffn2_w_down):
    bsz, seq, d = x.shape
    assert d == D_MODEL and seq % TM_FFN == 0 and seq % TM_MIX == 0
    assert w_ada.shape[0] == 1, "single-layer problem"
    h = x.reshape(bsz * seq, d)
    mods3 = _adaln_mods(c, w_ada[0], b_ada[0]).reshape(bsz, N_MOD, d)

    h = _ffn(h, mods3, g_norm1, ffn1_w_gate[0], ffn1_w_up[0], ffn1_w_down[0],
             mod_row=0, seq=seq)

    bias_rows = jnp.repeat(b_spatial[0].T, SGU_GROUP_DIM, axis=1)
    h = _mixer(h, mods3, g_norm2, w_in[0].astype(BF16), g_sgu_ln, b_sgu_ln, w_spatial[0],
               bias_rows, jnp.tile(g_q[0], N_Q_HEADS)[None], jnp.tile(g_k[0], N_KV_HEADS)[None],
               attn_sinks[0], w_branch_a[0].astype(BF16), w_branch_b[0].astype(BF16),
               w_out[0].astype(BF16), seq=seq)

    h = _ffn(h, mods3, g_norm3, ffn2_w_gate[0], ffn2_w_up[0], ffn2_w_down[0],
             mod_row=6, seq=seq)
    return h.reshape(bsz, seq, d)
```

```python
import functools

import jax
import jax.numpy as jnp
from jax import lax
from jax.experimental import pallas as pl
from jax.experimental.pallas import tpu as pltpu

F32 = jnp.float32
BF16 = jnp.bfloat16

D_MODEL = 1024
D_FF = 2816
CHUNK = 128
D_A = 512
SGU_GROUPS = 4
SGU_GROUP_DIM = D_A // SGU_GROUPS
HEAD_DIM = 64
N_Q_HEADS = 8
N_KV_HEADS = 2
D_B = N_Q_HEADS * HEAD_DIM
D_KV = N_KV_HEADS * HEAD_DIM
N_MOD = 9
EPS = 1e-6
NEG = -1e30

COL_UV = 0
COL_QKV = 2 * D_A
COL_GATES = COL_QKV + D_B + 2 * D_KV
IN_COLS = COL_GATES + 2 * D_MODEL

LANES = 128
MXU_DIM = 256
HEADS_PER_VREG = LANES // HEAD_DIM

TM_FFN = 512
TM_MIX = 512
FF_CHUNK = MXU_DIM
VMEM_LIMIT_FFN = 58 * 1024 * 1024
VMEM_LIMIT_MIX = 48 * 1024 * 1024


def _const_spec(shape):
    zeros = (0,) * len(shape)
    return pl.BlockSpec(shape, lambda i: zeros, pipeline_mode=pl.Buffered(1))


def _rms_modulate(h, g, mod_ref, row):
    ms = jnp.mean(h * h, axis=-1, keepdims=True)
    y = h * lax.rsqrt(ms + EPS) * g
    shift = mod_ref[0, row:row + 1, :]
    scale = mod_ref[0, row + 1:row + 2, :]
    return y * (1.0 + scale) + shift


def _mods_kernel(c_ref, w_ref, b_ref, o_ref):
    cond = jax.nn.silu(c_ref[...]).astype(BF16)
    o_ref[...] = jnp.dot(cond, w_ref[...].astype(BF16),
                         preferred_element_type=F32) + b_ref[...]


def _adaln_mods(c, w_ada, b_ada):
    bsz = c.shape[0]
    n_out = w_ada.shape[1]
    bn = D_MODEL
    return pl.pallas_call(
        _mods_kernel,
        out_shape=jax.ShapeDtypeStruct((bsz, n_out), F32),
        grid=(n_out // bn,),
        in_specs=[pl.BlockSpec((bsz, D_MODEL), lambda j: (0, 0)),
                  pl.BlockSpec((D_MODEL, bn), lambda j: (0, j)),
                  pl.BlockSpec((1, bn), lambda j: (0, j))],
        out_specs=pl.BlockSpec((bsz, bn), lambda j: (0, j)),
        compiler_params=pltpu.CompilerParams(dimension_semantics=("arbitrary",)),
        name="adaln_mods",
    )(c, w_ada, b_ada.reshape(1, n_out))


def _ffn_kernel(h_ref, mod_ref, gn_ref, wg_ref, wu_ref, wd_ref, o_ref,
                xn_ref, act_ref, *, mod_row):
    xn_ref[...] = _rms_modulate(h_ref[...], gn_ref[...], mod_ref, mod_row)
    for c in range(D_FF // FF_CHUNK):
        cs = slice(c * FF_CHUNK, (c + 1) * FF_CHUNK)
        xn = xn_ref[...]
        g = jnp.dot(xn, wg_ref[:, cs], preferred_element_type=F32)
        u = jnp.dot(xn, wu_ref[:, cs], preferred_element_type=F32)
        act_ref[:, cs] = jax.nn.silu(g) * u
    ff = jnp.dot(act_ref[...], wd_ref[...], preferred_element_type=F32)
    gate = mod_ref[0, mod_row + 2:mod_row + 3, :]
    o_ref[...] = h_ref[...] + 0.5 * gate * ff


def _ffn(h2d, mods3, g_norm, w_gate, w_up, w_down, *, mod_row, seq):
    n_tok = h2d.shape[0]
    tiles_per_seq = seq // TM_FFN
    row_spec = pl.BlockSpec((TM_FFN, D_MODEL), lambda i: (i, 0))
    return pl.pallas_call(
        functools.partial(_ffn_kernel, mod_row=mod_row),
        out_shape=jax.ShapeDtypeStruct((n_tok, D_MODEL), F32),
        grid=(n_tok // TM_FFN,),
        in_specs=[row_spec,
                  pl.BlockSpec((1, N_MOD, D_MODEL), lambda i: (i // tiles_per_seq, 0, 0)),
                  _const_spec((1, D_MODEL)),
                  _const_spec((D_MODEL, D_FF)),
                  _const_spec((D_MODEL, D_FF)),
                  _const_spec((D_FF, D_MODEL))],
        out_specs=row_spec,
        scratch_shapes=[pltpu.VMEM((TM_FFN, D_MODEL), F32),
                        pltpu.VMEM((TM_FFN, D_FF), F32)],
        compiler_params=pltpu.CompilerParams(
            dimension_semantics=("arbitrary",), vmem_limit_bytes=VMEM_LIMIT_FFN),
        name=f"ffn_mod{mod_row}",
    )(h2d, mods3, g_norm, w_gate, w_up, w_down)


def _head_rms(x, gain):
    lane = lax.broadcasted_iota(jnp.int32, (x.shape[0], LANES), 1)
    left = lane < HEAD_DIM
    cols = []
    for t in range(x.shape[1] // LANES):
        xt = x[:, t * LANES:(t + 1) * LANES]
        sq = xt * xt
        s_all = jnp.sum(sq, axis=-1, keepdims=True)
        s_left = jnp.sum(jnp.where(left, sq, 0.0), axis=-1, keepdims=True)
        ms = jnp.where(left, s_left, s_all - s_left) * (1.0 / HEAD_DIM)
        cols.append(xt * lax.rsqrt(ms + EPS))
    return jnp.concatenate(cols, axis=1) * gain


def _pair_blocks(x):
    lane = lax.broadcasted_iota(jnp.int32, x.shape, 1)
    left = lane < HEAD_DIM
    xr = pltpu.roll(x, HEAD_DIM, axis=1)
    blk0 = jnp.concatenate([jnp.where(left, x, 0.0), jnp.where(left, 0.0, xr)], axis=0)
    blk1 = jnp.concatenate([jnp.where(left, xr, 0.0), jnp.where(left, 0.0, x)], axis=0)
    return blk0.astype(BF16), blk1.astype(BF16)


def _mixer_kernel(sink_ref, h_ref, mod_ref, gn_ref, win_ref, gln_ref, bln_ref,
                  ws_ref, bs_ref, gq_ref, gk_ref, wa_ref, wb_ref, wo_ref, o_ref,
                  xn_ref, uv_ref, vn_ref, ya_ref, qkv_ref, qn_ref, attn_ref, k_ref, v_ref,
                  sg_ref, mg_ref, *, tiles_per_seq):
    tm = h_ref.shape[0]
    n_blk = tm // CHUNK
    first_tile = (pl.program_id(0) % tiles_per_seq) == 0
    blocks = [slice(b * CHUNK, (b + 1) * CHUNK) for b in range(n_blk)]

    @pl.when(first_tile)
    def _():
        k_ref[0:CHUNK, :] = jnp.zeros((CHUNK, D_KV), F32)
        v_ref[0:CHUNK, :] = jnp.zeros((CHUNK, D_KV), F32)

    xn_ref[...] = _rms_modulate(h_ref[...], gn_ref[...], mod_ref, 3).astype(BF16)

    n_gate_slices = 2 * n_blk
    gate_w = 2 * D_MODEL // n_gate_slices
    gate_slices = iter(range(n_gate_slices))

    def gate_slice():
        c0 = next(gate_slices) * gate_w
        sg_ref[:, c0:c0 + gate_w] = jax.nn.sigmoid(jnp.dot(
            xn_ref[...], win_ref[:, COL_GATES + c0:COL_GATES + c0 + gate_w],
            preferred_element_type=F32))

    qkv_ref[...] = jnp.dot(xn_ref[...], win_ref[:, COL_QKV:COL_QKV + D_B + 2 * D_KV],
                           preferred_element_type=F32)
    uv_ref[...] = jnp.dot(xn_ref[...], win_ref[:, COL_UV:COL_UV + 2 * D_A],
                          preferred_element_type=F32)

    def mixer_a_block(rs):
        v = jax.nn.gelu(uv_ref[rs, D_A:])
        mu = jnp.mean(v, axis=-1, keepdims=True)
        var = jnp.mean(jnp.square(v - mu), axis=-1, keepdims=True)
        vn_ref[rs, :] = ((v - mu) * lax.rsqrt(var + EPS) * gln_ref[...] + bln_ref[...]).astype(BF16)
        uv_ref[rs, :D_A] = jax.nn.gelu(uv_ref[rs, :D_A])

    def spatial_gate():
        t_idx = lax.broadcasted_iota(jnp.int32, (CHUNK, CHUNK), 0)
        s_idx = lax.broadcasted_iota(jnp.int32, (CHUNK, CHUNK), 1)
        causal = s_idx <= t_idx
        for g in range(SGU_GROUPS):
            gs = slice(g * SGU_GROUP_DIM, (g + 1) * SGU_GROUP_DIM)
            w_g = jnp.where(causal, ws_ref[g], 0.0).astype(BF16)
            v_cat = jnp.concatenate([vn_ref[rs, gs] for rs in blocks], axis=1)
            z = jnp.dot(w_g, v_cat, preferred_element_type=F32)
            for b, rs in enumerate(blocks):
                zb = z[:, b * SGU_GROUP_DIM:(b + 1) * SGU_GROUP_DIM] + bs_ref[:, gs]
                ya_ref[rs, gs] = (uv_ref[rs, gs] * zb).astype(BF16)

    for b, rs in enumerate(blocks):
        qn_ref[rs, :] = (_head_rms(qkv_ref[rs, :D_B], gq_ref[...])
                         * (HEAD_DIM ** -0.5)).astype(BF16)
        k_ref[(b + 1) * CHUNK:(b + 2) * CHUNK, :] = _head_rms(
            qkv_ref[rs, D_B:D_B + D_KV], gk_ref[...])
        v_ref[(b + 1) * CHUNK:(b + 2) * CHUNK, :] = qkv_ref[rs, D_B + D_KV:]

    r_idx = lax.broadcasted_iota(jnp.int32, (CHUNK, 4 * CHUNK), 0)
    c_idx = lax.broadcasted_iota(jnp.int32, (CHUNK, 4 * CHUNK), 1) & (2 * CHUNK - 1)
    band = c_idx - r_idx - 1
    band_first = band - jnp.where(c_idx < CHUNK, jnp.where(first_tile, 8 * CHUNK, 0), 0)
    lane = lax.broadcasted_iota(jnp.int32, (CHUNK, LANES), 1)
    left = lane < HEAD_DIM
    e_head = lax.broadcasted_iota(jnp.int32, (4 * CHUNK, LANES), 0) // (2 * CHUNK)
    e_lane = lax.broadcasted_iota(jnp.int32, (4 * CHUNK, LANES), 1) // HEAD_DIM
    ones_cols = jnp.where(e_head == e_lane, 1.0, 0.0).astype(BF16)
    tiles_per_kv = (N_Q_HEADS // N_KV_HEADS) // HEADS_PER_VREG

    def attention_block(b, rs):
        kblk = _pair_blocks(k_ref[b * CHUNK:(b + 2) * CHUNK, :])
        vblk = _pair_blocks(v_ref[b * CHUNK:(b + 2) * CHUNK, :])
        valid = lax.bitcast_convert_type(band_first if b == 0 else band,
                                         jnp.uint32) < jnp.uint32(CHUNK)
        for j in range(N_KV_HEADS):
            tiles = [j * tiles_per_kv + t for t in range(tiles_per_kv)]
            q_rows = jnp.concatenate([qn_ref[rs, t * LANES:(t + 1) * LANES] for t in tiles], axis=0)
            s = lax.dot_general(q_rows, kblk[j], (((1,), (1,)), ((), ())),
                                preferred_element_type=F32)
            probs, sink_terms = [], []
            for i, t in enumerate(tiles):
                st = jnp.where(valid, s[i * CHUNK:(i + 1) * CHUNK, :], NEG)
                e_heads, e_sinks = [], []
                for hh in range(HEADS_PER_VREG):
                    sink = sink_ref[t * HEADS_PER_VREG + hh]
                    sh = st[:, hh * 2 * CHUNK:(hh + 1) * 2 * CHUNK]
                    m = jnp.maximum(jnp.max(sh, axis=-1, keepdims=True), sink)
                    e_heads.append(jnp.exp(sh - m).astype(BF16))
                    e_sinks.append(jnp.exp(sink - m))
                probs.append(jnp.concatenate(e_heads, axis=1))
                sink_terms.append(jnp.where(left, e_sinks[0], e_sinks[1]))
            pv = jnp.dot(jnp.concatenate(probs, axis=0),
                         jnp.concatenate([vblk[j], ones_cols], axis=1),
                         preferred_element_type=F32)
            out = pv[:, :LANES] / (pv[:, LANES:] + jnp.concatenate(sink_terms, axis=0))
            for i, t in enumerate(tiles):
                attn_ref[rs, t * LANES:(t + 1) * LANES] = out[i * CHUNK:(i + 1) * CHUNK].astype(BF16)

    for b, rs in enumerate(blocks):
        mixer_a_block(rs)
        gate_slice()
        attention_block(b, rs)
        gate_slice()
    spatial_gate()

    k_ref[0:CHUNK, :] = k_ref[tm:tm + CHUNK, :]
    v_ref[0:CHUNK, :] = v_ref[tm:tm + CHUNK, :]

    for c in range(D_MODEL // MXU_DIM):
        cs = slice(c * MXU_DIM, (c + 1) * MXU_DIM)
        cs_b = slice(D_MODEL + c * MXU_DIM, D_MODEL + (c + 1) * MXU_DIM)
        y_a = jnp.dot(ya_ref[...], wa_ref[:, cs], preferred_element_type=F32)
        y_b = jnp.dot(attn_ref[...], wb_ref[:, cs], preferred_element_type=F32)
        mg_ref[:, cs] = (sg_ref[:, cs] * y_a + sg_ref[:, cs_b] * y_b).astype(BF16)
    out = jnp.dot(mg_ref[...], wo_ref[...], preferred_element_type=F32)
    o_ref[...] = h_ref[...] + mod_ref[0, 5:6, :] * out


def _mixer(h2d, mods3, g_norm, w_in, g_ln, b_ln, w_spatial, bias_rows, g_q, g_k, sinks,
           w_a, w_b, w_out, *, seq):
    n_tok = h2d.shape[0]
    tiles_per_seq = seq // TM_MIX
    row_spec = pl.BlockSpec((TM_MIX, D_MODEL), lambda i: (i, 0))
    return pl.pallas_call(
        functools.partial(_mixer_kernel, tiles_per_seq=tiles_per_seq),
        out_shape=jax.ShapeDtypeStruct((n_tok, D_MODEL), F32),
        grid=(n_tok // TM_MIX,),
        in_specs=[pl.BlockSpec(memory_space=pltpu.SMEM),
                  row_spec,
                  pl.BlockSpec((1, N_MOD, D_MODEL), lambda i: (i // tiles_per_seq, 0, 0)),
                  _const_spec((1, D_MODEL)),
                  _const_spec((D_MODEL, IN_COLS)),
                  _const_spec((1, D_A)),
                  _const_spec((1, D_A)),
                  _const_spec((SGU_GROUPS, CHUNK, CHUNK)),
                  _const_spec((CHUNK, D_A)),
                  _const_spec((1, D_B)),
                  _const_spec((1, D_KV)),
                  _const_spec((D_A, D_MODEL)),
                  _const_spec((D_B, D_MODEL)),
                  _const_spec((D_MODEL, D_MODEL))],
        out_specs=row_spec,
        scratch_shapes=[pltpu.VMEM((TM_MIX, D_MODEL), BF16),
                        pltpu.VMEM((TM_MIX, 2 * D_A), F32),
                        pltpu.VMEM((TM_MIX, D_A), BF16),
                        pltpu.VMEM((TM_MIX, D_A), BF16),
                        pltpu.VMEM((TM_MIX, D_B + 2 * D_KV), F32),
                        pltpu.VMEM((TM_MIX, D_B), BF16),
                        pltpu.VMEM((TM_MIX, D_B), BF16),
                        pltpu.VMEM((TM_MIX + CHUNK, D_KV), F32),
                        pltpu.VMEM((TM_MIX + CHUNK, D_KV), F32),
                        pltpu.VMEM((TM_MIX, 2 * D_MODEL), F32),
                        pltpu.VMEM((TM_MIX, D_MODEL), BF16)],
        compiler_params=pltpu.CompilerParams(
            dimension_semantics=("arbitrary",), vmem_limit_bytes=VMEM_LIMIT_MIX),
        name="mixer",
    )(sinks, h2d, mods3, g_norm, w_in, g_ln, b_ln, w_spatial, bias_rows, g_q, g_k,
      w_a, w_b, w_out)


def kernel(x, c, w_ada, b_ada, g_norm1, ffn1_w_gate, ffn1_w_up, ffn1_w_down, g_norm2, w_in,
           g_sgu_ln, b_sgu_ln, w_spatial, b_spatial, g_q, g_k, attn_sinks, w_branch_a,
           w_branch_b, w_out, g_norm3, ffn2_w_gate, ffn2_w_up, ffn2_w_down):
    bsz, seq, d = x.shape
    assert d == D_MODEL and seq % TM_FFN == 0 and seq % TM_MIX == 0
    assert w_ada.shape[0] == 1, "single-layer problem"
    h = x.reshape(bsz * seq, d)
    mods3 = _adaln_mods(c, w_ada[0], b_ada[0]).reshape(bsz, N_MOD, d)

    h = _ffn(h, mods3, g_norm1, ffn1_w_gate[0], ffn1_w_up[0], ffn1_w_down[0],
             mod_row=0, seq=seq)

    bias_rows = jnp.repeat(b_spatial[0].T, SGU_GROUP_DIM, axis=1)
    h = _mixer(h, mods3, g_norm2, w_in[0].astype(BF16), g_sgu_ln, b_sgu_ln, w_spatial[0],
               bias_rows, jnp.tile(g_q[0], N_Q_HEADS)[None], jnp.tile(g_k[0], N_KV_HEADS)[None],
               attn_sinks[0], w_branch_a[0].astype(BF16), w_branch_b[0].astype(BF16),
               w_out[0].astype(BF16), seq=seq)

    h = _ffn(h, mods3, g_norm3, ffn2_w_gate[0], ffn2_w_up[0], ffn2_w_down[0],
             mod_row=6, seq=seq)
    return h.reshape(bsz, seq, d)
```

```python
import functools

import jax
import jax.numpy as jnp
from jax import lax
from jax.experimental import pallas as pl
from jax.experimental.pallas import tpu as pltpu

F32 = jnp.float32
BF16 = jnp.bfloat16

D_MODEL = 1024
D_FF = 2816
CHUNK = 128
D_A = 512
SGU_GROUPS = 4
SGU_GROUP_DIM = D_A // SGU_GROUPS
HEAD_DIM = 64
N_Q_HEADS = 8
N_KV_HEADS = 2
D_B = N_Q_HEADS * HEAD_DIM
D_KV = N_KV_HEADS * HEAD_DIM
N_MOD = 9
EPS = 1e-6
NEG = -1e30

COL_UV = 0
COL_QKV = 2 * D_A
COL_GATES = COL_QKV + D_B + 2 * D_KV
IN_COLS = COL_GATES + 2 * D_MODEL

LANES = 128
MXU_DIM = 256
HEADS_PER_VREG = LANES // HEAD_DIM

TM_FFN = 512
TM_MIX = 512
FF_CHUNK = MXU_DIM
VMEM_LIMIT_FFN = 58 * 1024 * 1024
VMEM_LIMIT_MIX = 48 * 1024 * 1024


def _const_spec(shape):
    zeros = (0,) * len(shape)
    return pl.BlockSpec(shape, lambda i: zeros, pipeline_mode=pl.Buffered(1))


def _rms_modulate(h, g, mod_ref, row):
    ms = jnp.mean(h * h, axis=-1, keepdims=True)
    y = h * lax.rsqrt(ms + EPS) * g
    shift = mod_ref[0, row:row + 1, :]
    scale = mod_ref[0, row + 1:row + 2, :]
    return y * (1.0 + scale) + shift


def _mods_kernel(c_ref, w_ref, b_ref, o_ref):
    cond = jax.nn.silu(c_ref[...]).astype(BF16)
    o_ref[...] = jnp.dot(cond, w_ref[...].astype(BF16),
                         preferred_element_type=F32) + b_ref[...]


def _adaln_mods(c, w_ada, b_ada):
    bsz = c.shape[0]
    n_out = w_ada.shape[1]
    bn = D_MODEL
    return pl.pallas_call(
        _mods_kernel,
        out_shape=jax.ShapeDtypeStruct((bsz, n_out), F32),
        grid=(n_out // bn,),
        in_specs=[pl.BlockSpec((bsz, D_MODEL), lambda j: (0, 0)),
                  pl.BlockSpec((D_MODEL, bn), lambda j: (0, j)),
                  pl.BlockSpec((1, bn), lambda j: (0, j))],
        out_specs=pl.BlockSpec((bsz, bn), lambda j: (0, j)),
        compiler_params=pltpu.CompilerParams(dimension_semantics=("arbitrary",)),
        name="adaln_mods",
    )(c, w_ada, b_ada.reshape(1, n_out))


def _ffn_kernel(h_ref, mod_ref, shift_ref, gn_ref, wg_ref, wu_ref, wd_ref, o_ref,
                xn_ref, ha_ref, act_ref, sproj_ref, *, mod_row, tiles_per_seq):
    step = pl.program_id(0)
    c0 = slice(0, FF_CHUNK)

    @pl.when(step == 0)
    def _():
        sproj_ref[:, :FF_CHUNK] = jnp.dot(shift_ref[...], wg_ref[:, c0],
                                          preferred_element_type=F32)
        sproj_ref[:, FF_CHUNK:] = jnp.dot(shift_ref[...], wu_ref[:, c0],
                                          preferred_element_type=F32)

    h = h_ref[...]
    shift = mod_ref[0, mod_row:mod_row + 1, :]
    ha_ref[...] = h * (gn_ref[...] * (1.0 + mod_ref[0, mod_row + 1:mod_row + 2, :]))
    r = lax.rsqrt(jnp.mean(h * h, axis=-1, keepdims=True) + EPS)
    seq_row = pl.ds(step // tiles_per_seq, 1)
    ha = ha_ref[...]
    g = r * jnp.dot(ha, wg_ref[:, c0], preferred_element_type=F32) + sproj_ref[seq_row, :FF_CHUNK]
    u = r * jnp.dot(ha, wu_ref[:, c0], preferred_element_type=F32) + sproj_ref[seq_row, FF_CHUNK:]
    act_ref[:, c0] = jax.nn.silu(g) * u
    xn_ref[...] = ha_ref[...] * r + shift
    for c in range(1, D_FF // FF_CHUNK):
        cs = slice(c * FF_CHUNK, (c + 1) * FF_CHUNK)
        xn = xn_ref[...]
        g = jnp.dot(xn, wg_ref[:, cs], preferred_element_type=F32)
        u = jnp.dot(xn, wu_ref[:, cs], preferred_element_type=F32)
        act_ref[:, cs] = jax.nn.silu(g) * u
    ff = jnp.dot(act_ref[...], wd_ref[...], preferred_element_type=F32)
    gate = mod_ref[0, mod_row + 2:mod_row + 3, :]
    o_ref[...] = h_ref[...] + 0.5 * gate * ff


def _ffn(h2d, mods3, g_norm, w_gate, w_up, w_down, *, mod_row, seq):
    n_tok = h2d.shape[0]
    bsz = mods3.shape[0]
    tiles_per_seq = seq // TM_FFN
    row_spec = pl.BlockSpec((TM_FFN, D_MODEL), lambda i: (i, 0))
    return pl.pallas_call(
        functools.partial(_ffn_kernel, mod_row=mod_row, tiles_per_seq=tiles_per_seq),
        out_shape=jax.ShapeDtypeStruct((n_tok, D_MODEL), F32),
        grid=(n_tok // TM_FFN,),
        in_specs=[row_spec,
                  pl.BlockSpec((1, N_MOD, D_MODEL), lambda i: (i // tiles_per_seq, 0, 0)),
                  _const_spec((bsz, D_MODEL)),
                  _const_spec((1, D_MODEL)),
                  _const_spec((D_MODEL, D_FF)),
                  _const_spec((D_MODEL, D_FF)),
                  _const_spec((D_FF, D_MODEL))],
        out_specs=row_spec,
        scratch_shapes=[pltpu.VMEM((TM_FFN, D_MODEL), F32),
                        pltpu.VMEM((TM_FFN, D_MODEL), F32),
                        pltpu.VMEM((TM_FFN, D_FF), F32),
                        pltpu.VMEM((bsz, 2 * FF_CHUNK), F32)],
        compiler_params=pltpu.CompilerParams(
            dimension_semantics=("arbitrary",), vmem_limit_bytes=VMEM_LIMIT_FFN),
        name=f"ffn_mod{mod_row}",
    )(h2d, mods3, mods3[:, mod_row, :], g_norm, w_gate, w_up, w_down)


def _head_rms(x, gain):
    lane = lax.broadcasted_iota(jnp.int32, (x.shape[0], LANES), 1)
    left = lane < HEAD_DIM
    cols = []
    for t in range(x.shape[1] // LANES):
        xt = x[:, t * LANES:(t + 1) * LANES]
        sq = xt * xt
        s_all = jnp.sum(sq, axis=-1, keepdims=True)
        s_left = jnp.sum(jnp.where(left, sq, 0.0), axis=-1, keepdims=True)
        ms = jnp.where(left, s_left, s_all - s_left) * (1.0 / HEAD_DIM)
        cols.append(xt * lax.rsqrt(ms + EPS))
    return jnp.concatenate(cols, axis=1) * gain


def _pair_blocks(x):
    lane = lax.broadcasted_iota(jnp.int32, x.shape, 1)
    left = lane < HEAD_DIM
    xr = pltpu.roll(x, HEAD_DIM, axis=1)
    blk0 = jnp.concatenate([jnp.where(left, x, 0.0), jnp.where(left, 0.0, xr)], axis=0)
    blk1 = jnp.concatenate([jnp.where(left, xr, 0.0), jnp.where(left, 0.0, x)], axis=0)
    return blk0.astype(BF16), blk1.astype(BF16)


def _mixer_kernel(sink_ref, h_ref, mod_ref, shift_ref, gn_ref, win_ref, gln_ref, bln_ref,
                  ws_ref, bs_ref, gq_ref, gk_ref, wa_ref, wb_ref, wo_ref, o_ref,
                  xn_ref, uv_ref, vn_ref, ya_ref, qkv_ref, qn_ref, attn_ref, k_ref, v_ref,
                  sg_ref, mg_ref, ha_ref, sproj_ref, *, tiles_per_seq):
    tm = h_ref.shape[0]
    n_blk = tm // CHUNK
    first_tile = (pl.program_id(0) % tiles_per_seq) == 0
    blocks = [slice(b * CHUNK, (b + 1) * CHUNK) for b in range(n_blk)]

    @pl.when(first_tile)
    def _():
        k_ref[0:CHUNK, :] = jnp.zeros((CHUNK, D_KV), F32)
        v_ref[0:CHUNK, :] = jnp.zeros((CHUNK, D_KV), F32)

    qkv_cols = slice(COL_QKV, COL_QKV + D_B + 2 * D_KV)

    @pl.when(pl.program_id(0) == 0)
    def _():
        sproj_ref[...] = jnp.dot(shift_ref[...].astype(BF16), win_ref[:, qkv_cols],
                                 preferred_element_type=F32)

    h = h_ref[...]
    ha = h * (gn_ref[...] * (1.0 + mod_ref[0, 4:5, :]))
    ha_ref[...] = ha.astype(BF16)
    r = lax.rsqrt(jnp.mean(h * h, axis=-1, keepdims=True) + EPS)
    seq_row = pl.ds(pl.program_id(0) // tiles_per_seq, 1)
    qkv_ref[...] = r * jnp.dot(ha_ref[...], win_ref[:, qkv_cols],
                               preferred_element_type=F32) + sproj_ref[seq_row, :]
    xn_ref[...] = (ha * r + mod_ref[0, 3:4, :]).astype(BF16)

    n_gate_slices = 2 * n_blk
    gate_w = 2 * D_MODEL // n_gate_slices
    gate_slices = iter(range(n_gate_slices))

    def gate_slice():
        c0 = next(gate_slices) * gate_w
        sg_ref[:, c0:c0 + gate_w] = jax.nn.sigmoid(jnp.dot(
            xn_ref[...], win_ref[:, COL_GATES + c0:COL_GATES + c0 + gate_w],
            preferred_element_type=F32))

    uv_ref[...] = jnp.dot(xn_ref[...], win_ref[:, COL_UV:COL_UV + 2 * D_A],
                          preferred_element_type=F32)

    def mixer_a_block(rs):
        v = jax.nn.gelu(uv_ref[rs, D_A:])
        mu = jnp.mean(v, axis=-1, keepdims=True)
        var = jnp.mean(jnp.square(v - mu), axis=-1, keepdims=True)
        vn_ref[rs, :] = ((v - mu) * lax.rsqrt(var + EPS) * gln_ref[...] + bln_ref[...]).astype(BF16)
        uv_ref[rs, :D_A] = jax.nn.gelu(uv_ref[rs, :D_A])

    def spatial_gate():
        t_idx = lax.broadcasted_iota(jnp.int32, (CHUNK, CHUNK), 0)
        s_idx = lax.broadcasted_iota(jnp.int32, (CHUNK, CHUNK), 1)
        causal = s_idx <= t_idx
        for g in range(SGU_GROUPS):
            gs = slice(g * SGU_GROUP_DIM, (g + 1) * SGU_GROUP_DIM)
            w_g = jnp.where(causal, ws_ref[g], 0.0).astype(BF16)
            v_cat = jnp.concatenate([vn_ref[rs, gs] for rs in blocks], axis=1)
            z = jnp.dot(w_g, v_cat, preferred_element_type=F32)
            for b, rs in enumerate(blocks):
                zb = z[:, b * SGU_GROUP_DIM:(b + 1) * SGU_GROUP_DIM] + bs_ref[:, gs]
                ya_ref[rs, gs] = (uv_ref[rs, gs] * zb).astype(BF16)

    for b, rs in enumerate(blocks):
        qn_ref[rs, :] = (_head_rms(qkv_ref[rs, :D_B], gq_ref[...])
                         * (HEAD_DIM ** -0.5)).astype(BF16)
        k_ref[(b + 1) * CHUNK:(b + 2) * CHUNK, :] = _head_rms(
            qkv_ref[rs, D_B:D_B + D_KV], gk_ref[...])
        v_ref[(b + 1) * CHUNK:(b + 2) * CHUNK, :] = qkv_ref[rs, D_B + D_KV:]

    r_idx = lax.broadcasted_iota(jnp.int32, (CHUNK, 4 * CHUNK), 0)
    c_idx = lax.broadcasted_iota(jnp.int32, (CHUNK, 4 * CHUNK), 1) & (2 * CHUNK - 1)
    band = c_idx - r_idx - 1
    band_first = band - jnp.where(c_idx < CHUNK, jnp.where(first_tile, 8 * CHUNK, 0), 0)
    lane = lax.broadcasted_iota(jnp.int32, (CHUNK, LANES), 1)
    left = lane < HEAD_DIM
    e_head = lax.broadcasted_iota(jnp.int32, (4 * CHUNK, LANES), 0) // (2 * CHUNK)
    e_lane = lax.broadcasted_iota(jnp.int32, (4 * CHUNK, LANES), 1) // HEAD_DIM
    ones_cols = jnp.where(e_head == e_lane, 1.0, 0.0).astype(BF16)
    tiles_per_kv = (N_Q_HEADS // N_KV_HEADS) // HEADS_PER_VREG

    def attention_block(b, rs):
        kblk = _pair_blocks(k_ref[b * CHUNK:(b + 2) * CHUNK, :])
        vblk = _pair_blocks(v_ref[b * CHUNK:(b + 2) * CHUNK, :])
        valid = lax.bitcast_convert_type(band_first if b == 0 else band,
                                         jnp.uint32) < jnp.uint32(CHUNK)
        for j in range(N_KV_HEADS):
            tiles = [j * tiles_per_kv + t for t in range(tiles_per_kv)]
            q_rows = jnp.concatenate([qn_ref[rs, t * LANES:(t + 1) * LANES] for t in tiles], axis=0)
            s = lax.dot_general(q_rows, kblk[j], (((1,), (1,)), ((), ())),
                                preferred_element_type=F32)
            probs, sink_terms = [], []
            for i, t in enumerate(tiles):
                st = jnp.where(valid, s[i * CHUNK:(i + 1) * CHUNK, :], NEG)
                e_heads, e_sinks = [], []
                for hh in range(HEADS_PER_VREG):
                    sink = sink_ref[t * HEADS_PER_VREG + hh]
                    sh = st[:, hh * 2 * CHUNK:(hh + 1) * 2 * CHUNK]
                    m = jnp.maximum(jnp.max(sh, axis=-1, keepdims=True), sink)
                    e_heads.append(jnp.exp(sh - m).astype(BF16))
                    e_sinks.append(jnp.exp(sink - m))
                probs.append(jnp.concatenate(e_heads, axis=1))
                sink_terms.append(jnp.where(left, e_sinks[0], e_sinks[1]))
            pv = jnp.dot(jnp.concatenate(probs, axis=0),
                         jnp.concatenate([vblk[j], ones_cols], axis=1),
                         preferred_element_type=F32)
            out = pv[:, :LANES] / (pv[:, LANES:] + jnp.concatenate(sink_terms, axis=0))
            for i, t in enumerate(tiles):
                attn_ref[rs, t * LANES:(t + 1) * LANES] = out[i * CHUNK:(i + 1) * CHUNK].astype(BF16)

    for b, rs in enumerate(blocks):
        mixer_a_block(rs)
        gate_slice()
        attention_block(b, rs)
        gate_slice()
    spatial_gate()

    k_ref[0:CHUNK, :] = k_ref[tm:tm + CHUNK, :]
    v_ref[0:CHUNK, :] = v_ref[tm:tm + CHUNK, :]

    for c in range(D_MODEL // MXU_DIM):
        cs = slice(c * MXU_DIM, (c + 1) * MXU_DIM)
        cs_b = slice(D_MODEL + c * MXU_DIM, D_MODEL + (c + 1) * MXU_DIM)
        y_a = jnp.dot(ya_ref[...], wa_ref[:, cs], preferred_element_type=F32)
        y_b = jnp.dot(attn_ref[...], wb_ref[:, cs], preferred_element_type=F32)
        mg_ref[:, cs] = (sg_ref[:, cs] * y_a + sg_ref[:, cs_b] * y_b).astype(BF16)
    out = jnp.dot(mg_ref[...], wo_ref[...], preferred_element_type=F32)
    o_ref[...] = h_ref[...] + mod_ref[0, 5:6, :] * out


def _mixer(h2d, mods3, g_norm, w_in, g_ln, b_ln, w_spatial, bias_rows, g_q, g_k, sinks,
           w_a, w_b, w_out, *, seq):
    n_tok = h2d.shape[0]
    bsz = mods3.shape[0]
    tiles_per_seq = seq // TM_MIX
    row_spec = pl.BlockSpec((TM_MIX, D_MODEL), lambda i: (i, 0))
    return pl.pallas_call(
        functools.partial(_mixer_kernel, tiles_per_seq=tiles_per_seq),
        out_shape=jax.ShapeDtypeStruct((n_tok, D_MODEL), F32),
        grid=(n_tok // TM_MIX,),
        in_specs=[pl.BlockSpec(memory_space=pltpu.SMEM),
                  row_spec,
                  pl.BlockSpec((1, N_MOD, D_MODEL), lambda i: (i // tiles_per_seq, 0, 0)),
                  _const_spec((bsz, D_MODEL)),
                  _const_spec((1, D_MODEL)),
                  _const_spec((D_MODEL, IN_COLS)),
                  _const_spec((1, D_A)),
                  _const_spec((1, D_A)),
                  _const_spec((SGU_GROUPS, CHUNK, CHUNK)),
                  _const_spec((CHUNK, D_A)),
                  _const_spec((1, D_B)),
                  _const_spec((1, D_KV)),
                  _const_spec((D_A, D_MODEL)),
                  _const_spec((D_B, D_MODEL)),
                  _const_spec((D_MODEL, D_MODEL))],
        out_specs=row_spec,
        scratch_shapes=[pltpu.VMEM((TM_MIX, D_MODEL), BF16),
                        pltpu.VMEM((TM_MIX, 2 * D_A), F32),
                        pltpu.VMEM((TM_MIX, D_A), BF16),
                        pltpu.VMEM((TM_MIX, D_A), BF16),
                        pltpu.VMEM((TM_MIX, D_B + 2 * D_KV), F32),
                        pltpu.VMEM((TM_MIX, D_B), BF16),
                        pltpu.VMEM((TM_MIX, D_B), BF16),
                        pltpu.VMEM((TM_MIX + CHUNK, D_KV), F32),
                        pltpu.VMEM((TM_MIX + CHUNK, D_KV), F32),
                        pltpu.VMEM((TM_MIX, 2 * D_MODEL), F32),
                        pltpu.VMEM((TM_MIX, D_MODEL), BF16),
                        pltpu.VMEM((TM_MIX, D_MODEL), BF16),
                        pltpu.VMEM((bsz, D_B + 2 * D_KV), F32)],
        compiler_params=pltpu.CompilerParams(
            dimension_semantics=("arbitrary",), vmem_limit_bytes=VMEM_LIMIT_MIX),
        name="mixer",
    )(sinks, h2d, mods3, mods3[:, 3, :], g_norm, w_in, g_ln, b_ln, w_spatial, bias_rows,
      g_q, g_k, w_a, w_b, w_out)


def kernel(x, c, w_ada, b_ada, g_norm1, ffn1_w_gate, ffn1_w_up, ffn1_w_down, g_norm2, w_in,
           g_sgu_ln, b_sgu_ln, w_spatial, b_spatial, g_q, g_k, attn_sinks, w_branch_a,
           w_branch_b, w_out, g_norm3, ffn2_w_gate, ffn2_w_up, ffn2_w_down):
    bsz, seq, d = x.shape
    assert d == D_MODEL and seq % TM_FFN == 0 and seq % TM_MIX == 0
    assert w_ada.shape[0] == 1, "single-layer problem"
    h = x.reshape(bsz * seq, d)
    mods3 = _adaln_mods(c, w_ada[0], b_ada[0]).reshape(bsz, N_MOD, d)

    h = _ffn(h, mods3, g_norm1, ffn1_w_gate[0], ffn1_w_up[0], ffn1_w_down[0],
             mod_row=0, seq=seq)

    bias_rows = jnp.repeat(b_spatial[0].T, SGU_GROUP_DIM, axis=1)
    h = _mixer(h, mods3, g_norm2, w_in[0].astype(BF16), g_sgu_ln, b_sgu_ln, w_spatial[0],
               bias_rows, jnp.tile(g_q[0], N_Q_HEADS)[None], jnp.tile(g_k[0], N_KV_HEADS)[None],
               attn_sinks[0], w_branch_a[0].astype(BF16), w_branch_b[0].astype(BF16),
               w_out[0].astype(BF16), seq=seq)

    h = _ffn(h, mods3, g_norm3, ffn2_w_gate[0], ffn2_w_up[0], ffn2_w_down[0],
             mod_row=6, seq=seq)
    return h.reshape(bsz, seq, d)
```

```python
import functools

import jax
import jax.numpy as jnp
from jax import lax
from jax.experimental import pallas as pl
from jax.experimental.pallas import tpu as pltpu

F32 = jnp.float32
BF16 = jnp.bfloat16

D_MODEL = 1024
D_FF = 2816
CHUNK = 128
D_A = 512
SGU_GROUPS = 4
SGU_GROUP_DIM = D_A // SGU_GROUPS
HEAD_DIM = 64
N_Q_HEADS = 8
N_KV_HEADS = 2
D_B = N_Q_HEADS * HEAD_DIM
D_KV = N_KV_HEADS * HEAD_DIM
N_MOD = 9
EPS = 1e-6
NEG = -1e30

COL_UV = 0
COL_QKV = 2 * D_A
COL_GATES = COL_QKV + D_B + 2 * D_KV
IN_COLS = COL_GATES + 2 * D_MODEL

LANES = 128
MXU_DIM = 256
HEADS_PER_VREG = LANES // HEAD_DIM
BF16_SUBLANES = 16

TM_FFN = 512
TM_MIX = 512
FF_CHUNK = MXU_DIM
VMEM_LIMIT_FFN = 58 * 1024 * 1024
VMEM_LIMIT_MIX = 48 * 1024 * 1024


def _const_spec(shape):
    zeros = (0,) * len(shape)
    return pl.BlockSpec(shape, lambda i: zeros, pipeline_mode=pl.Buffered(1))


def _dot(x, w):
    return lax.dot_general(x, w, (((1,), (0,)), ((), ())), preferred_element_type=F32)


def _cast_rider_specs(weights, n_steps):
    specs, shapes = [], []
    for w in weights:
        n_rows, n_cols = w.shape
        rows = BF16_SUBLANES
        while n_rows % rows or n_rows // rows > n_steps:
            rows += BF16_SUBLANES
        last = n_rows // rows - 1
        specs.append(pl.BlockSpec((rows, n_cols), lambda i, last=last: (jnp.minimum(i, last), 0)))
        shapes.append(jax.ShapeDtypeStruct(w.shape, BF16))
    return specs, shapes


def _run_cast_riders(in_refs, out_refs):
    for src, dst in zip(in_refs, out_refs):
        dst[...] = src[...].astype(BF16)


def _mods_kernel(c_ref, w_ref, b_ref, o_ref):
    o_ref[0] = _dot(jax.nn.silu(c_ref[...]), w_ref[...]) + b_ref[...]


def _adaln_mods(c, w_ada, b_ada2, n_rows):
    bsz = c.shape[0]
    return pl.pallas_call(
        _mods_kernel,
        out_shape=jax.ShapeDtypeStruct((n_rows, bsz, D_MODEL), F32),
        grid=(n_rows,),
        in_specs=[pl.BlockSpec((bsz, D_MODEL), lambda j: (0, 0)),
                  pl.BlockSpec((D_MODEL, D_MODEL), lambda j: (0, j)),
                  pl.BlockSpec((1, D_MODEL), lambda j: (0, j))],
        out_specs=pl.BlockSpec((1, bsz, D_MODEL), lambda j: (j, 0, 0)),
        compiler_params=pltpu.CompilerParams(dimension_semantics=("arbitrary",)),
        name="adaln_mods",
    )(c, w_ada, b_ada2)


def _ffn_kernel(*refs, mod_row, tiles_per_seq, n_cast):
    refs = list(refs)
    h_ref, mods_ref, gn_ref, wg_ref, wu_ref, wd_ref = refs[:6]
    del refs[:6]
    cast_in = refs[:n_cast]
    del refs[:n_cast]
    o_ref = refs.pop(0)
    cast_out = refs[:n_cast]
    del refs[:n_cast]
    xn_ref, ha_ref, act_ref, sproj_ref = refs

    step = pl.program_id(0)
    c0 = slice(0, FF_CHUNK)

    @pl.when(step == 0)
    def _():
        sproj_ref[:, :FF_CHUNK] = _dot(mods_ref[mod_row], wg_ref[:, c0])
        sproj_ref[:, FF_CHUNK:] = _dot(mods_ref[mod_row], wu_ref[:, c0])

    _run_cast_riders(cast_in, cast_out)

    seq_row = pl.ds(step // tiles_per_seq, 1)
    h = h_ref[...]
    shift = mods_ref[mod_row, seq_row, :]
    ha_ref[...] = h * (gn_ref[...] * (1.0 + mods_ref[mod_row + 1, seq_row, :]))
    r = lax.rsqrt(jnp.mean(h * h, axis=-1, keepdims=True) + EPS)
    ha = ha_ref[...]
    g = r * _dot(ha, wg_ref[:, c0]) + sproj_ref[seq_row, :FF_CHUNK]
    u = r * _dot(ha, wu_ref[:, c0]) + sproj_ref[seq_row, FF_CHUNK:]
    act_ref[:, c0] = jax.nn.silu(g) * u
    xn_ref[...] = ha_ref[...] * r + shift
    for c in range(1, D_FF // FF_CHUNK):
        cs = slice(c * FF_CHUNK, (c + 1) * FF_CHUNK)
        xn = xn_ref[...]
        g = _dot(xn, wg_ref[:, cs])
        u = _dot(xn, wu_ref[:, cs])
        act_ref[:, cs] = jax.nn.silu(g) * u
    ff = _dot(act_ref[...], wd_ref[...])
    gate = mods_ref[mod_row + 2, seq_row, :]
    o_ref[...] = h_ref[...] + 0.5 * gate * ff


def _ffn(h2d, mods, g_norm, w_gate, w_up, w_down, *, mod_row, seq, cast=()):
    n_tok = h2d.shape[0]
    n_mod_rows, bsz, _ = mods.shape
    n_steps = n_tok // TM_FFN
    tiles_per_seq = seq // TM_FFN
    row_spec = pl.BlockSpec((TM_FFN, D_MODEL), lambda i: (i, 0))
    cast_specs, cast_shapes = _cast_rider_specs(cast, n_steps)
    in_specs = [row_spec,
                _const_spec((n_mod_rows, bsz, D_MODEL)),
                _const_spec((1, D_MODEL)),
                _const_spec(w_gate.shape),
                _const_spec(w_up.shape),
                _const_spec(w_down.shape)] + cast_specs
    out_specs = [row_spec] + cast_specs
    out_shapes = [jax.ShapeDtypeStruct((n_tok, D_MODEL), F32)] + cast_shapes
    operands = [h2d, mods, g_norm, w_gate, w_up, w_down, *cast]
    return pl.pallas_call(
        functools.partial(_ffn_kernel, mod_row=mod_row, tiles_per_seq=tiles_per_seq,
                          n_cast=len(cast)),
        out_shape=out_shapes,
        grid=(n_steps,),
        in_specs=in_specs,
        out_specs=out_specs,
        scratch_shapes=[pltpu.VMEM((TM_FFN, D_MODEL), F32),
                        pltpu.VMEM((TM_FFN, D_MODEL), F32),
                        pltpu.VMEM((TM_FFN, D_FF), F32),
                        pltpu.VMEM((bsz, 2 * FF_CHUNK), F32)],
        compiler_params=pltpu.CompilerParams(
            dimension_semantics=("arbitrary",), vmem_limit_bytes=VMEM_LIMIT_FFN),
        name=f"ffn_mod{mod_row}",
    )(*operands)


def _head_rms(x, gain):
    lane = lax.broadcasted_iota(jnp.int32, (x.shape[0], LANES), 1)
    left = lane < HEAD_DIM
    cols = []
    for t in range(x.shape[1] // LANES):
        xt = x[:, t * LANES:(t + 1) * LANES]
        sq = xt * xt
        s_all = jnp.sum(sq, axis=-1, keepdims=True)
        s_left = jnp.sum(jnp.where(left, sq, 0.0), axis=-1, keepdims=True)
        ms = jnp.where(left, s_left, s_all - s_left) * (1.0 / HEAD_DIM)
        cols.append(xt * lax.rsqrt(ms + EPS))
    return jnp.concatenate(cols, axis=1) * gain


def _pair_blocks(x):
    lane = lax.broadcasted_iota(jnp.int32, x.shape, 1)
    left = lane < HEAD_DIM
    xr = pltpu.roll(x, HEAD_DIM, axis=1)
    blk0 = jnp.concatenate([jnp.where(left, x, 0.0), jnp.where(left, 0.0, xr)], axis=0)
    blk1 = jnp.concatenate([jnp.where(left, xr, 0.0), jnp.where(left, 0.0, x)], axis=0)
    return blk0.astype(BF16), blk1.astype(BF16)


def _mixer_kernel(*refs, mod_row, tiles_per_seq, n_cast):
    refs = list(refs)
    (sink_ref, h_ref, mods_ref, gn_ref, win_ref, gln_ref, bln_ref, ws_ref, bs_ref, gq_ref, gk_ref,
     wa_ref, wb_ref, wo_ref) = refs[:14]
    del refs[:14]
    cast_in = refs[:n_cast]
    del refs[:n_cast]
    o_ref = refs.pop(0)
    cast_out = refs[:n_cast]
    del refs[:n_cast]
    (xn_ref, uv_ref, vn_ref, ya_ref, qkv_ref, qn_ref, attn_ref, k_ref, v_ref,
     sg_ref, mg_ref, ha_ref, sproj_ref) = refs
    tm = h_ref.shape[0]
    n_blk = tm // CHUNK
    first_tile = (pl.program_id(0) % tiles_per_seq) == 0
    blocks = [slice(b * CHUNK, (b + 1) * CHUNK) for b in range(n_blk)]

    @pl.when(first_tile)
    def _():
        k_ref[0:CHUNK, :] = jnp.zeros((CHUNK, D_KV), F32)
        v_ref[0:CHUNK, :] = jnp.zeros((CHUNK, D_KV), F32)

    qkv_cols = slice(COL_QKV, COL_QKV + D_B + 2 * D_KV)

    @pl.when(pl.program_id(0) == 0)
    def _():
        sproj_ref[...] = _dot(mods_ref[mod_row].astype(BF16), win_ref[:, qkv_cols])

    _run_cast_riders(cast_in, cast_out)

    seq_row = pl.ds(pl.program_id(0) // tiles_per_seq, 1)
    h = h_ref[...]
    ha = h * (gn_ref[...] * (1.0 + mods_ref[mod_row + 1, seq_row, :]))
    ha_ref[...] = ha.astype(BF16)
    r = lax.rsqrt(jnp.mean(h * h, axis=-1, keepdims=True) + EPS)
    qkv_ref[...] = r * jnp.dot(ha_ref[...], win_ref[:, qkv_cols],
                               preferred_element_type=F32) + sproj_ref[seq_row, :]
    xn_ref[...] = (ha * r + mods_ref[mod_row, seq_row, :]).astype(BF16)

    n_gate_slices = 2 * n_blk
    gate_w = 2 * D_MODEL // n_gate_slices
    gate_slices = iter(range(n_gate_slices))

    def gate_slice():
        c0 = next(gate_slices) * gate_w
        sg_ref[:, c0:c0 + gate_w] = jax.nn.sigmoid(jnp.dot(
            xn_ref[...], win_ref[:, COL_GATES + c0:COL_GATES + c0 + gate_w],
            preferred_element_type=F32))

    uv_ref[...] = jnp.dot(xn_ref[...], win_ref[:, COL_UV:COL_UV + 2 * D_A],
                          preferred_element_type=F32)

    def mixer_a_block(rs):
        v = jax.nn.gelu(uv_ref[rs, D_A:])
        mu = jnp.mean(v, axis=-1, keepdims=True)
        var = jnp.mean(jnp.square(v - mu), axis=-1, keepdims=True)
        vn_ref[rs, :] = ((v - mu) * lax.rsqrt(var + EPS) * gln_ref[...] + bln_ref[...]).astype(BF16)
        uv_ref[rs, :D_A] = jax.nn.gelu(uv_ref[rs, :D_A])

    def spatial_gate():
        t_idx = lax.broadcasted_iota(jnp.int32, (CHUNK, CHUNK), 0)
        s_idx = lax.broadcasted_iota(jnp.int32, (CHUNK, CHUNK), 1)
        causal = s_idx <= t_idx
        for g in range(SGU_GROUPS):
            gs = slice(g * SGU_GROUP_DIM, (g + 1) * SGU_GROUP_DIM)
            w_g = jnp.where(causal, ws_ref[g], 0.0).astype(BF16)
            v_cat = jnp.concatenate([vn_ref[rs, gs] for rs in blocks], axis=1)
            z = jnp.dot(w_g, v_cat, preferred_element_type=F32)
            for b, rs in enumerate(blocks):
                zb = z[:, b * SGU_GROUP_DIM:(b + 1) * SGU_GROUP_DIM] + bs_ref[:, gs]
                ya_ref[rs, gs] = (uv_ref[rs, gs] * zb).astype(BF16)

    for b, rs in enumerate(blocks):
        qn_ref[rs, :] = (_head_rms(qkv_ref[rs, :D_B], gq_ref[...])
                         * (HEAD_DIM ** -0.5)).astype(BF16)
        k_ref[(b + 1) * CHUNK:(b + 2) * CHUNK, :] = _head_rms(
            qkv_ref[rs, D_B:D_B + D_KV], gk_ref[...])
        v_ref[(b + 1) * CHUNK:(b + 2) * CHUNK, :] = qkv_ref[rs, D_B + D_KV:]

    r_idx = lax.broadcasted_iota(jnp.int32, (CHUNK, 4 * CHUNK), 0)
    c_idx = lax.broadcasted_iota(jnp.int32, (CHUNK, 4 * CHUNK), 1) & (2 * CHUNK - 1)
    band = c_idx - r_idx - 1
    band_first = band - jnp.where(c_idx < CHUNK, jnp.where(first_tile, 8 * CHUNK, 0), 0)
    lane = lax.broadcasted_iota(jnp.int32, (CHUNK, LANES), 1)
    left = lane < HEAD_DIM
    e_head = lax.broadcasted_iota(jnp.int32, (4 * CHUNK, LANES), 0) // (2 * CHUNK)
    e_lane = lax.broadcasted_iota(jnp.int32, (4 * CHUNK, LANES), 1) // HEAD_DIM
    ones_cols = jnp.where(e_head == e_lane, 1.0, 0.0).astype(BF16)
    tiles_per_kv = (N_Q_HEADS // N_KV_HEADS) // HEADS_PER_VREG

    def attention_block(b, rs):
        kblk = _pair_blocks(k_ref[b * CHUNK:(b + 2) * CHUNK, :])
        vblk = _pair_blocks(v_ref[b * CHUNK:(b + 2) * CHUNK, :])
        valid = lax.bitcast_convert_type(band_first if b == 0 else band,
                                         jnp.uint32) < jnp.uint32(CHUNK)
        for j in range(N_KV_HEADS):
            tiles = [j * tiles_per_kv + t for t in range(tiles_per_kv)]
            q_rows = jnp.concatenate([qn_ref[rs, t * LANES:(t + 1) * LANES] for t in tiles], axis=0)
            s = lax.dot_general(q_rows, kblk[j], (((1,), (1,)), ((), ())),
                                preferred_element_type=F32)
            probs, sink_terms = [], []
            for i, t in enumerate(tiles):
                st = jnp.where(valid, s[i * CHUNK:(i + 1) * CHUNK, :], NEG)
                e_heads, e_sinks = [], []
                for hh in range(HEADS_PER_VREG):
                    sink = sink_ref[t * HEADS_PER_VREG + hh]
                    sh = st[:, hh * 2 * CHUNK:(hh + 1) * 2 * CHUNK]
                    m = jnp.maximum(jnp.max(sh, axis=-1, keepdims=True), sink)
                    e_heads.append(jnp.exp(sh - m).astype(BF16))
                    e_sinks.append(jnp.exp(sink - m))
                probs.append(jnp.concatenate(e_heads, axis=1))
                sink_terms.append(jnp.where(left, e_sinks[0], e_sinks[1]))
            pv = jnp.dot(jnp.concatenate(probs, axis=0),
                         jnp.concatenate([vblk[j], ones_cols], axis=1),
                         preferred_element_type=F32)
            out = pv[:, :LANES] / (pv[:, LANES:] + jnp.concatenate(sink_terms, axis=0))
            for i, t in enumerate(tiles):
                attn_ref[rs, t * LANES:(t + 1) * LANES] = out[i * CHUNK:(i + 1) * CHUNK].astype(BF16)

    for b, rs in enumerate(blocks):
        mixer_a_block(rs)
        gate_slice()
        attention_block(b, rs)
        gate_slice()
    spatial_gate()

    k_ref[0:CHUNK, :] = k_ref[tm:tm + CHUNK, :]
    v_ref[0:CHUNK, :] = v_ref[tm:tm + CHUNK, :]

    for c in range(D_MODEL // MXU_DIM):
        cs = slice(c * MXU_DIM, (c + 1) * MXU_DIM)
        cs_b = slice(D_MODEL + c * MXU_DIM, D_MODEL + (c + 1) * MXU_DIM)
        y_a = jnp.dot(ya_ref[...], wa_ref[:, cs], preferred_element_type=F32)
        y_b = jnp.dot(attn_ref[...], wb_ref[:, cs], preferred_element_type=F32)
        mg_ref[:, cs] = (sg_ref[:, cs] * y_a + sg_ref[:, cs_b] * y_b).astype(BF16)
    out = jnp.dot(mg_ref[...], wo_ref[...], preferred_element_type=F32)
    o_ref[...] = h_ref[...] + mods_ref[mod_row + 2, seq_row, :] * out


def _mixer(h2d, mods, g_norm, w_in, g_ln, b_ln, w_spatial, bias_rows, g_q, g_k, sinks,
           w_a, w_b, w_out, *, mod_row, seq, cast=()):
    n_tok = h2d.shape[0]
    n_mod_rows, bsz, _ = mods.shape
    n_steps = n_tok // TM_MIX
    tiles_per_seq = seq // TM_MIX
    row_spec = pl.BlockSpec((TM_MIX, D_MODEL), lambda i: (i, 0))
    cast_specs, cast_shapes = _cast_rider_specs(cast, n_steps)
    return pl.pallas_call(
        functools.partial(_mixer_kernel, mod_row=mod_row, tiles_per_seq=tiles_per_seq,
                          n_cast=len(cast)),
        out_shape=[jax.ShapeDtypeStruct((n_tok, D_MODEL), F32)] + cast_shapes,
        grid=(n_steps,),
        in_specs=[pl.BlockSpec(memory_space=pltpu.SMEM),
                  row_spec,
                  _const_spec((n_mod_rows, bsz, D_MODEL)),
                  _const_spec((1, D_MODEL)),
                  _const_spec((D_MODEL, IN_COLS)),
                  _const_spec((1, D_A)),
                  _const_spec((1, D_A)),
                  _const_spec((SGU_GROUPS, CHUNK, CHUNK)),
                  _const_spec((CHUNK, D_A)),
                  _const_spec((1, D_B)),
                  _const_spec((1, D_KV)),
                  _const_spec((D_A, D_MODEL)),
                  _const_spec((D_B, D_MODEL)),
                  _const_spec((D_MODEL, D_MODEL))] + cast_specs,
        out_specs=[row_spec] + cast_specs,
        scratch_shapes=[pltpu.VMEM((TM_MIX, D_MODEL), BF16),
                        pltpu.VMEM((TM_MIX, 2 * D_A), F32),
                        pltpu.VMEM((TM_MIX, D_A), BF16),
                        pltpu.VMEM((TM_MIX, D_A), BF16),
                        pltpu.VMEM((TM_MIX, D_B + 2 * D_KV), F32),
                        pltpu.VMEM((TM_MIX, D_B), BF16),
                        pltpu.VMEM((TM_MIX, D_B), BF16),
                        pltpu.VMEM((TM_MIX + CHUNK, D_KV), F32),
                        pltpu.VMEM((TM_MIX + CHUNK, D_KV), F32),
                        pltpu.VMEM((TM_MIX, 2 * D_MODEL), F32),
                        pltpu.VMEM((TM_MIX, D_MODEL), BF16),
                        pltpu.VMEM((TM_MIX, D_MODEL), BF16),
                        pltpu.VMEM((bsz, D_B + 2 * D_KV), F32)],
        compiler_params=pltpu.CompilerParams(
            dimension_semantics=("arbitrary",), vmem_limit_bytes=VMEM_LIMIT_MIX),
        name="mixer",
    )(sinks, h2d, mods, g_norm, w_in, g_ln, b_ln, w_spatial, bias_rows, g_q, g_k,
      w_a, w_b, w_out, *cast)


def kernel(x, c, w_ada, b_ada, g_norm1, ffn1_w_gate, ffn1_w_up, ffn1_w_down, g_norm2, w_in,
           g_sgu_ln, b_sgu_ln, w_spatial, b_spatial, g_q, g_k, attn_sinks, w_branch_a,
           w_branch_b, w_out, g_norm3, ffn2_w_gate, ffn2_w_up, ffn2_w_down):
    bsz, seq, d = x.shape
    assert d == D_MODEL and seq % TM_FFN == 0 and seq % TM_MIX == 0
    assert w_ada.shape[0] == 1, "single-layer problem"
    h = x.reshape(bsz * seq, d)
    b_ada2 = b_ada[0].reshape(1, N_MOD * d)

    mods = _adaln_mods(c, w_ada[0], b_ada2, N_MOD)

    h, w_in_b, w_a_b, w_b_b, w_out_b = _ffn(
        h, mods, g_norm1, ffn1_w_gate[0], ffn1_w_up[0], ffn1_w_down[0], mod_row=0, seq=seq,
        cast=(w_in[0], w_branch_a[0], w_branch_b[0], w_out[0]))

    bias_rows = jnp.repeat(b_spatial[0].T, SGU_GROUP_DIM, axis=1)
    h, w_gate_b, w_up_b, w_down_b = _mixer(
        h, mods, g_norm2, w_in_b, g_sgu_ln, b_sgu_ln, w_spatial[0], bias_rows,
        jnp.tile(g_q[0], N_Q_HEADS)[None], jnp.tile(g_k[0], N_KV_HEADS)[None], attn_sinks[0],
        w_a_b, w_b_b, w_out_b, mod_row=3, seq=seq,
        cast=(ffn2_w_gate[0], ffn2_w_up[0], ffn2_w_down[0]))

    (h,) = _ffn(h, mods, g_norm3, w_gate_b, w_up_b, w_down_b, mod_row=6, seq=seq)
    return h.reshape(bsz, seq, d)
```

```python
import functools

import jax
import jax.numpy as jnp
from jax import lax
from jax.experimental import pallas as pl
from jax.experimental.pallas import tpu as pltpu

F32 = jnp.float32
BF16 = jnp.bfloat16

D_MODEL = 1024
D_FF = 2816
CHUNK = 128
D_A = 512
SGU_GROUPS = 4
SGU_GROUP_DIM = D_A // SGU_GROUPS
HEAD_DIM = 64
N_Q_HEADS = 8
N_KV_HEADS = 2
D_B = N_Q_HEADS * HEAD_DIM
D_KV = N_KV_HEADS * HEAD_DIM
N_MOD = 9
EPS = 1e-6
NEG = -1e30

COL_UV = 0
COL_QKV = 2 * D_A
COL_GATES = COL_QKV + D_B + 2 * D_KV
IN_COLS = COL_GATES + 2 * D_MODEL

LANES = 128
MXU_DIM = 256
HEADS_PER_VREG = LANES // HEAD_DIM
BF16_SUBLANES = 16

TM_FFN_F32_WEIGHTS = 512
TM_FFN_BF16_WEIGHTS = 1024
TM_MIX = 512
FF_CHUNK = MXU_DIM
VMEM_LIMIT_FFN = 58 * 1024 * 1024
VMEM_LIMIT_MIX = 48 * 1024 * 1024


def _const_spec(shape):
    zeros = (0,) * len(shape)
    return pl.BlockSpec(shape, lambda i: zeros, pipeline_mode=pl.Buffered(1))


def _dot(x, w):
    return lax.dot_general(x, w, (((1,), (0,)), ((), ())), preferred_element_type=F32)


def _cast_rider_specs(weights, n_steps):
    specs, shapes = [], []
    for w in weights:
        n_rows, n_cols = w.shape
        rows = BF16_SUBLANES
        while n_rows % rows or n_rows // rows > n_steps:
            rows += BF16_SUBLANES
        last = n_rows // rows - 1
        specs.append(pl.BlockSpec((rows, n_cols), lambda i, last=last: (jnp.minimum(i, last), 0)))
        shapes.append(jax.ShapeDtypeStruct(w.shape, BF16))
    return specs, shapes


def _run_cast_riders(in_refs, out_refs):
    for src, dst in zip(in_refs, out_refs):
        dst[...] = src[...].astype(BF16)


def _mods_kernel(c_ref, w_ref, b_ref, o_ref):
    o_ref[0] = _dot(jax.nn.silu(c_ref[...]), w_ref[...]) + b_ref[...]


def _adaln_mods(c, w_ada, b_ada2, n_rows):
    bsz = c.shape[0]
    return pl.pallas_call(
        _mods_kernel,
        out_shape=jax.ShapeDtypeStruct((n_rows, bsz, D_MODEL), F32),
        grid=(n_rows,),
        in_specs=[pl.BlockSpec((bsz, D_MODEL), lambda j: (0, 0)),
                  pl.BlockSpec((D_MODEL, D_MODEL), lambda j: (0, j)),
                  pl.BlockSpec((1, D_MODEL), lambda j: (0, j))],
        out_specs=pl.BlockSpec((1, bsz, D_MODEL), lambda j: (j, 0, 0)),
        compiler_params=pltpu.CompilerParams(dimension_semantics=("arbitrary",)),
        name="adaln_mods",
    )(c, w_ada, b_ada2)


def _ffn_kernel(*refs, mod_row, tiles_per_seq, n_cast):
    refs = list(refs)
    h_ref, mods_ref, gn_ref, wg_ref, wu_ref, wd_ref = refs[:6]
    del refs[:6]
    cast_in = refs[:n_cast]
    del refs[:n_cast]
    o_ref = refs.pop(0)
    cast_out = refs[:n_cast]
    del refs[:n_cast]
    xn_ref, ha_ref, act_ref, sproj_ref = refs

    step = pl.program_id(0)
    c0 = slice(0, FF_CHUNK)

    @pl.when(step == 0)
    def _():
        sproj_ref[:, :FF_CHUNK] = _dot(mods_ref[mod_row], wg_ref[:, c0])
        sproj_ref[:, FF_CHUNK:] = _dot(mods_ref[mod_row], wu_ref[:, c0])

    _run_cast_riders(cast_in, cast_out)

    seq_row = pl.ds(step // tiles_per_seq, 1)
    h = h_ref[...]
    shift = mods_ref[mod_row, seq_row, :]
    ha_ref[...] = h * (gn_ref[...] * (1.0 + mods_ref[mod_row + 1, seq_row, :]))
    r = lax.rsqrt(jnp.mean(h * h, axis=-1, keepdims=True) + EPS)
    ha = ha_ref[...]
    g = r * _dot(ha, wg_ref[:, c0]) + sproj_ref[seq_row, :FF_CHUNK]
    u = r * _dot(ha, wu_ref[:, c0]) + sproj_ref[seq_row, FF_CHUNK:]
    act_ref[:, c0] = jax.nn.silu(g) * u
    xn_ref[...] = ha_ref[...] * r + shift
    for c in range(1, D_FF // FF_CHUNK):
        cs = slice(c * FF_CHUNK, (c + 1) * FF_CHUNK)
        xn = xn_ref[...]
        g = _dot(xn, wg_ref[:, cs])
        u = _dot(xn, wu_ref[:, cs])
        act_ref[:, cs] = jax.nn.silu(g) * u
    ff = _dot(act_ref[...], wd_ref[...])
    gate = mods_ref[mod_row + 2, seq_row, :]
    o_ref[...] = h_ref[...] + 0.5 * gate * ff


def _ffn(h2d, mods, g_norm, w_gate, w_up, w_down, *, mod_row, seq, cast=()):
    n_tok = h2d.shape[0]
    n_mod_rows, bsz, _ = mods.shape
    tm = TM_FFN_BF16_WEIGHTS if w_gate.dtype == BF16 else TM_FFN_F32_WEIGHTS
    assert seq % tm == 0
    n_steps = n_tok // tm
    tiles_per_seq = seq // tm
    row_spec = pl.BlockSpec((tm, D_MODEL), lambda i: (i, 0))
    cast_specs, cast_shapes = _cast_rider_specs(cast, n_steps)
    in_specs = [row_spec,
                _const_spec((n_mod_rows, bsz, D_MODEL)),
                _const_spec((1, D_MODEL)),
                _const_spec(w_gate.shape),
                _const_spec(w_up.shape),
                _const_spec(w_down.shape)] + cast_specs
    out_specs = [row_spec] + cast_specs
    out_shapes = [jax.ShapeDtypeStruct((n_tok, D_MODEL), F32)] + cast_shapes
    operands = [h2d, mods, g_norm, w_gate, w_up, w_down, *cast]
    return pl.pallas_call(
        functools.partial(_ffn_kernel, mod_row=mod_row, tiles_per_seq=tiles_per_seq,
                          n_cast=len(cast)),
        out_shape=out_shapes,
        grid=(n_steps,),
        in_specs=in_specs,
        out_specs=out_specs,
        scratch_shapes=[pltpu.VMEM((tm, D_MODEL), F32),
                        pltpu.VMEM((tm, D_MODEL), F32),
                        pltpu.VMEM((tm, D_FF), F32),
                        pltpu.VMEM((bsz, 2 * FF_CHUNK), F32)],
        compiler_params=pltpu.CompilerParams(
            dimension_semantics=("arbitrary",), vmem_limit_bytes=VMEM_LIMIT_FFN),
        name=f"ffn_mod{mod_row}",
    )(*operands)


def _head_rms(x, gain):
    lane = lax.broadcasted_iota(jnp.int32, (x.shape[0], LANES), 1)
    left = lane < HEAD_DIM
    cols = []
    for t in range(x.shape[1] // LANES):
        xt = x[:, t * LANES:(t + 1) * LANES]
        sq = xt * xt
        s_all = jnp.sum(sq, axis=-1, keepdims=True)
        s_left = jnp.sum(jnp.where(left, sq, 0.0), axis=-1, keepdims=True)
        ms = jnp.where(left, s_left, s_all - s_left) * (1.0 / HEAD_DIM)
        cols.append(xt * lax.rsqrt(ms + EPS))
    return jnp.concatenate(cols, axis=1) * gain


def _pair_blocks(x):
    lane = lax.broadcasted_iota(jnp.int32, x.shape, 1)
    left = lane < HEAD_DIM
    xr = pltpu.roll(x, HEAD_DIM, axis=1)
    blk0 = jnp.concatenate([jnp.where(left, x, 0.0), jnp.where(left, 0.0, xr)], axis=0)
    blk1 = jnp.concatenate([jnp.where(left, xr, 0.0), jnp.where(left, 0.0, x)], axis=0)
    return blk0.astype(BF16), blk1.astype(BF16)


def _mixer_kernel(*refs, mod_row, tiles_per_seq, n_cast):
    refs = list(refs)
    (sink_ref, h_ref, mods_ref, gn_ref, win_ref, gln_ref, bln_ref, ws_ref, bs_ref, gq_ref, gk_ref,
     wa_ref, wb_ref, wo_ref) = refs[:14]
    del refs[:14]
    cast_in = refs[:n_cast]
    del refs[:n_cast]
    o_ref = refs.pop(0)
    cast_out = refs[:n_cast]
    del refs[:n_cast]
    (xn_ref, uv_ref, vn_ref, ya_ref, qkv_ref, qn_ref, attn_ref, k_ref, v_ref,
     sg_ref, mg_ref, ha_ref, sproj_ref) = refs
    tm = h_ref.shape[0]
    n_blk = tm // CHUNK
    first_tile = (pl.program_id(0) % tiles_per_seq) == 0
    blocks = [slice(b * CHUNK, (b + 1) * CHUNK) for b in range(n_blk)]

    @pl.when(first_tile)
    def _():
        k_ref[0:CHUNK, :] = jnp.zeros((CHUNK, D_KV), F32)
        v_ref[0:CHUNK, :] = jnp.zeros((CHUNK, D_KV), F32)

    qkv_cols = slice(COL_QKV, COL_QKV + D_B + 2 * D_KV)

    @pl.when(pl.program_id(0) == 0)
    def _():
        sproj_ref[...] = _dot(mods_ref[mod_row].astype(BF16), win_ref[:, qkv_cols])

    _run_cast_riders(cast_in, cast_out)

    seq_row = pl.ds(pl.program_id(0) // tiles_per_seq, 1)
    h = h_ref[...]
    ha = h * (gn_ref[...] * (1.0 + mods_ref[mod_row + 1, seq_row, :]))
    ha_ref[...] = ha.astype(BF16)
    r = lax.rsqrt(jnp.mean(h * h, axis=-1, keepdims=True) + EPS)
    qkv_ref[...] = r * jnp.dot(ha_ref[...], win_ref[:, qkv_cols],
                               preferred_element_type=F32) + sproj_ref[seq_row, :]
    xn_ref[...] = (ha * r + mods_ref[mod_row, seq_row, :]).astype(BF16)

    n_gate_slices = 2 * n_blk
    gate_w = 2 * D_MODEL // n_gate_slices
    gate_slices = iter(range(n_gate_slices))

    def gate_slice():
        c0 = next(gate_slices) * gate_w
        sg_ref[:, c0:c0 + gate_w] = jax.nn.sigmoid(jnp.dot(
            xn_ref[...], win_ref[:, COL_GATES + c0:COL_GATES + c0 + gate_w],
            preferred_element_type=F32))

    uv_ref[...] = jnp.dot(xn_ref[...], win_ref[:, COL_UV:COL_UV + 2 * D_A],
                          preferred_element_type=F32)

    def mixer_a_block(rs):
        v = jax.nn.gelu(uv_ref[rs, D_A:])
        mu = jnp.mean(v, axis=-1, keepdims=True)
        var = jnp.mean(jnp.square(v - mu), axis=-1, keepdims=True)
        vn_ref[rs, :] = ((v - mu) * lax.rsqrt(var + EPS) * gln_ref[...] + bln_ref[...]).astype(BF16)
        uv_ref[rs, :D_A] = jax.nn.gelu(uv_ref[rs, :D_A])

    def spatial_gate():
        t_idx = lax.broadcasted_iota(jnp.int32, (CHUNK, CHUNK), 0)
        s_idx = lax.broadcasted_iota(jnp.int32, (CHUNK, CHUNK), 1)
        causal = s_idx <= t_idx
        for g in range(SGU_GROUPS):
            gs = slice(g * SGU_GROUP_DIM, (g + 1) * SGU_GROUP_DIM)
            w_g = jnp.where(causal, ws_ref[g], 0.0).astype(BF16)
            v_cat = jnp.concatenate([vn_ref[rs, gs] for rs in blocks], axis=1)
            z = jnp.dot(w_g, v_cat, preferred_element_type=F32)
            for b, rs in enumerate(blocks):
                zb = z[:, b * SGU_GROUP_DIM:(b + 1) * SGU_GROUP_DIM] + bs_ref[:, gs]
                ya_ref[rs, gs] = (uv_ref[rs, gs] * zb).astype(BF16)

    for b, rs in enumerate(blocks):
        qn_ref[rs, :] = (_head_rms(qkv_ref[rs, :D_B], gq_ref[...])
                         * (HEAD_DIM ** -0.5)).astype(BF16)
        k_ref[(b + 1) * CHUNK:(b + 2) * CHUNK, :] = _head_rms(
            qkv_ref[rs, D_B:D_B + D_KV], gk_ref[...])
        v_ref[(b + 1) * CHUNK:(b + 2) * CHUNK, :] = qkv_ref[rs, D_B + D_KV:]

    r_idx = lax.broadcasted_iota(jnp.int32, (CHUNK, 4 * CHUNK), 0)
    c_idx = lax.broadcasted_iota(jnp.int32, (CHUNK, 4 * CHUNK), 1) & (2 * CHUNK - 1)
    band = c_idx - r_idx - 1
    band_first = band - jnp.where(c_idx < CHUNK, jnp.where(first_tile, 8 * CHUNK, 0), 0)
    lane = lax.broadcasted_iota(jnp.int32, (CHUNK, LANES), 1)
    left = lane < HEAD_DIM
    e_head = lax.broadcasted_iota(jnp.int32, (4 * CHUNK, LANES), 0) // (2 * CHUNK)
    e_lane = lax.broadcasted_iota(jnp.int32, (4 * CHUNK, LANES), 1) // HEAD_DIM
    ones_cols = jnp.where(e_head == e_lane, 1.0, 0.0).astype(BF16)
    tiles_per_kv = (N_Q_HEADS // N_KV_HEADS) // HEADS_PER_VREG

    def attention_block(b, rs):
        kblk = _pair_blocks(k_ref[b * CHUNK:(b + 2) * CHUNK, :])
        vblk = _pair_blocks(v_ref[b * CHUNK:(b + 2) * CHUNK, :])
        valid = lax.bitcast_convert_type(band_first if b == 0 else band,
                                         jnp.uint32) < jnp.uint32(CHUNK)
        for j in range(N_KV_HEADS):
            tiles = [j * tiles_per_kv + t for t in range(tiles_per_kv)]
            q_rows = jnp.concatenate([qn_ref[rs, t * LANES:(t + 1) * LANES] for t in tiles], axis=0)
            s = lax.dot_general(q_rows, kblk[j], (((1,), (1,)), ((), ())),
                                preferred_element_type=F32)
            probs, sink_terms = [], []
            for i, t in enumerate(tiles):
                st = jnp.where(valid, s[i * CHUNK:(i + 1) * CHUNK, :], NEG)
                e_heads, e_sinks = [], []
                for hh in range(HEADS_PER_VREG):
                    sink = sink_ref[t * HEADS_PER_VREG + hh]
                    sh = st[:, hh * 2 * CHUNK:(hh + 1) * 2 * CHUNK]
                    m = jnp.maximum(jnp.max(sh, axis=-1, keepdims=True), sink)
                    e_heads.append(jnp.exp(sh - m).astype(BF16))
                    e_sinks.append(jnp.exp(sink - m))
                probs.append(jnp.concatenate(e_heads, axis=1))
                sink_terms.append(jnp.where(left, e_sinks[0], e_sinks[1]))
            pv = jnp.dot(jnp.concatenate(probs, axis=0),
                         jnp.concatenate([vblk[j], ones_cols], axis=1),
                         preferred_element_type=F32)
            out = pv[:, :LANES] / (pv[:, LANES:] + jnp.concatenate(sink_terms, axis=0))
            for i, t in enumerate(tiles):
                attn_ref[rs, t * LANES:(t + 1) * LANES] = out[i * CHUNK:(i + 1) * CHUNK].astype(BF16)

    for b, rs in enumerate(blocks):
        mixer_a_block(rs)
        gate_slice()
        attention_block(b, rs)
        gate_slice()
    spatial_gate()

    k_ref[0:CHUNK, :] = k_ref[tm:tm + CHUNK, :]
    v_ref[0:CHUNK, :] = v_ref[tm:tm + CHUNK, :]

    for c in range(D_MODEL // MXU_DIM):
        cs = slice(c * MXU_DIM, (c + 1) * MXU_DIM)
        cs_b = slice(D_MODEL + c * MXU_DIM, D_MODEL + (c + 1) * MXU_DIM)
        y_a = jnp.dot(ya_ref[...], wa_ref[:, cs], preferred_element_type=F32)
        y_b = jnp.dot(attn_ref[...], wb_ref[:, cs], preferred_element_type=F32)
        mg_ref[:, cs] = (sg_ref[:, cs] * y_a + sg_ref[:, cs_b] * y_b).astype(BF16)
    out = jnp.dot(mg_ref[...], wo_ref[...], preferred_element_type=F32)
    o_ref[...] = h_ref[...] + mods_ref[mod_row + 2, seq_row, :] * out


def _mixer(h2d, mods, g_norm, w_in, g_ln, b_ln, w_spatial, bias_rows, g_q, g_k, sinks,
           w_a, w_b, w_out, *, mod_row, seq, cast=()):
    n_tok = h2d.shape[0]
    n_mod_rows, bsz, _ = mods.shape
    n_steps = n_tok // TM_MIX
    tiles_per_seq = seq // TM_MIX
    row_spec = pl.BlockSpec((TM_MIX, D_MODEL), lambda i: (i, 0))
    cast_specs, cast_shapes = _cast_rider_specs(cast, n_steps)
    return pl.pallas_call(
        functools.partial(_mixer_kernel, mod_row=mod_row, tiles_per_seq=tiles_per_seq,
                          n_cast=len(cast)),
        out_shape=[jax.ShapeDtypeStruct((n_tok, D_MODEL), F32)] + cast_shapes,
        grid=(n_steps,),
        in_specs=[pl.BlockSpec(memory_space=pltpu.SMEM),
                  row_spec,
                  _const_spec((n_mod_rows, bsz, D_MODEL)),
                  _const_spec((1, D_MODEL)),
                  _const_spec((D_MODEL, IN_COLS)),
                  _const_spec((1, D_A)),
                  _const_spec((1, D_A)),
                  _const_spec((SGU_GROUPS, CHUNK, CHUNK)),
                  _const_spec((CHUNK, D_A)),
                  _const_spec((1, D_B)),
                  _const_spec((1, D_KV)),
                  _const_spec((D_A, D_MODEL)),
                  _const_spec((D_B, D_MODEL)),
                  _const_spec((D_MODEL, D_MODEL))] + cast_specs,
        out_specs=[row_spec] + cast_specs,
        scratch_shapes=[pltpu.VMEM((TM_MIX, D_MODEL), BF16),
                        pltpu.VMEM((TM_MIX, 2 * D_A), F32),
                        pltpu.VMEM((TM_MIX, D_A), BF16),
                        pltpu.VMEM((TM_MIX, D_A), BF16),
                        pltpu.VMEM((TM_MIX, D_B + 2 * D_KV), F32),
                        pltpu.VMEM((TM_MIX, D_B), BF16),
                        pltpu.VMEM((TM_MIX, D_B), BF16),
                        pltpu.VMEM((TM_MIX + CHUNK, D_KV), F32),
                        pltpu.VMEM((TM_MIX + CHUNK, D_KV), F32),
                        pltpu.VMEM((TM_MIX, 2 * D_MODEL), F32),
                        pltpu.VMEM((TM_MIX, D_MODEL), BF16),
                        pltpu.VMEM((TM_MIX, D_MODEL), BF16),
                        pltpu.VMEM((bsz, D_B + 2 * D_KV), F32)],
        compiler_params=pltpu.CompilerParams(
            dimension_semantics=("arbitrary",), vmem_limit_bytes=VMEM_LIMIT_MIX),
        name="mixer",
    )(sinks, h2d, mods, g_norm, w_in, g_ln, b_ln, w_spatial, bias_rows, g_q, g_k,
      w_a, w_b, w_out, *cast)


def kernel(x, c, w_ada, b_ada, g_norm1, ffn1_w_gate, ffn1_w_up, ffn1_w_down, g_norm2, w_in,
           g_sgu_ln, b_sgu_ln, w_spatial, b_spatial, g_q, g_k, attn_sinks, w_branch_a,
           w_branch_b, w_out, g_norm3, ffn2_w_gate, ffn2_w_up, ffn2_w_down):
    bsz, seq, d = x.shape
    assert d == D_MODEL and seq % TM_MIX == 0
    assert w_ada.shape[0] == 1, "single-layer problem"
    h = x.reshape(bsz * seq, d)
    b_ada2 = b_ada[0].reshape(1, N_MOD * d)

    mods = _adaln_mods(c, w_ada[0], b_ada2, N_MOD)

    h, w_in_b, w_a_b, w_b_b, w_out_b = _ffn(
        h, mods, g_norm1, ffn1_w_gate[0], ffn1_w_up[0], ffn1_w_down[0], mod_row=0, seq=seq,
        cast=(w_in[0], w_branch_a[0], w_branch_b[0], w_out[0]))

    bias_rows = jnp.repeat(b_spatial[0].T, SGU_GROUP_DIM, axis=1)
    h, w_gate_b, w_up_b, w_down_b = _mixer(
        h, mods, g_norm2, w_in_b, g_sgu_ln, b_sgu_ln, w_spatial[0], bias_rows,
        jnp.tile(g_q[0], N_Q_HEADS)[None], jnp.tile(g_k[0], N_KV_HEADS)[None], attn_sinks[0],
        w_a_b, w_b_b, w_out_b, mod_row=3, seq=seq,
        cast=(ffn2_w_gate[0], ffn2_w_up[0], ffn2_w_down[0]))

    (h,) = _ffn(h, mods, g_norm3, w_gate_b, w_up_b, w_down_b, mod_row=6, seq=seq)
    return h.reshape(bsz, seq, d)
```

```python
import functools

import jax
import jax.numpy as jnp
from jax import lax
from jax.experimental import pallas as pl
from jax.experimental.pallas import tpu as pltpu

F32 = jnp.float32
BF16 = jnp.bfloat16

D_MODEL = 1024
D_FF = 2816
CHUNK = 128
D_A = 512
SGU_GROUPS = 4
SGU_GROUP_DIM = D_A // SGU_GROUPS
HEAD_DIM = 64
N_Q_HEADS = 8
N_KV_HEADS = 2
D_B = N_Q_HEADS * HEAD_DIM
D_KV = N_KV_HEADS * HEAD_DIM
N_MOD = 9
EPS = 1e-6
NEG = -1e30

COL_UV = 0
COL_QKV = 2 * D_A
COL_GATES = COL_QKV + D_B + 2 * D_KV
IN_COLS = COL_GATES + 2 * D_MODEL

LANES = 128
MXU_DIM = 256
HEADS_PER_VREG = LANES // HEAD_DIM
BF16_SUBLANES = 16

TM_FFN_F32_WEIGHTS = 512
TM_FFN_BF16_WEIGHTS = 1024
TM_MIX = 512
FF_CHUNK = MXU_DIM
VMEM_LIMIT_FFN = 58 * 1024 * 1024
VMEM_LIMIT_MIX = 48 * 1024 * 1024


def _const_spec(shape):
    zeros = (0,) * len(shape)
    return pl.BlockSpec(shape, lambda i: zeros, pipeline_mode=pl.Buffered(1))


def _dot(x, w):
    return lax.dot_general(x, w, (((1,), (0,)), ((), ())), preferred_element_type=F32)


def _cast_rider_specs(weights, n_steps):
    specs, shapes = [], []
    for w in weights:
        n_rows, n_cols = w.shape
        rows = BF16_SUBLANES
        while n_rows % rows or n_rows // rows > n_steps:
            rows += BF16_SUBLANES
        last = n_rows // rows - 1
        specs.append(pl.BlockSpec((rows, n_cols), lambda i, last=last: (jnp.minimum(i, last), 0)))
        shapes.append(jax.ShapeDtypeStruct(w.shape, BF16))
    return specs, shapes


def _run_cast_riders(in_refs, out_refs):
    for src, dst in zip(in_refs, out_refs):
        dst[...] = src[...].astype(BF16)


def _mods_kernel(c_ref, w_ref, b_ref, o_ref):
    o_ref[0] = _dot(jax.nn.silu(c_ref[...]), w_ref[...]) + b_ref[...]


def _adaln_mods(c, w_ada, b_ada2, n_rows):
    bsz = c.shape[0]
    return pl.pallas_call(
        _mods_kernel,
        out_shape=jax.ShapeDtypeStruct((n_rows, bsz, D_MODEL), F32),
        grid=(n_rows,),
        in_specs=[pl.BlockSpec((bsz, D_MODEL), lambda j: (0, 0)),
                  pl.BlockSpec((D_MODEL, D_MODEL), lambda j: (0, j)),
                  pl.BlockSpec((1, D_MODEL), lambda j: (0, j))],
        out_specs=pl.BlockSpec((1, bsz, D_MODEL), lambda j: (j, 0, 0)),
        compiler_params=pltpu.CompilerParams(dimension_semantics=("arbitrary",)),
        name="adaln_mods",
    )(c, w_ada, b_ada2)


def _ffn_kernel(*refs, mod_row, tiles_per_seq, n_cast):
    refs = list(refs)
    h_ref, mods_ref, gn_ref, wg_ref, wu_ref, wd_ref = refs[:6]
    del refs[:6]
    cast_in = refs[:n_cast]
    del refs[:n_cast]
    o_ref = refs.pop(0)
    cast_out = refs[:n_cast]
    del refs[:n_cast]
    xn_ref, ha_ref, act_ref, sproj_ref = refs

    step = pl.program_id(0)
    c0 = slice(0, FF_CHUNK)

    @pl.when(step == 0)
    def _():
        sproj_ref[:, :FF_CHUNK] = _dot(mods_ref[mod_row], wg_ref[:, c0])
        sproj_ref[:, FF_CHUNK:] = _dot(mods_ref[mod_row], wu_ref[:, c0])

    _run_cast_riders(cast_in, cast_out)

    seq_row = pl.ds(step // tiles_per_seq, 1)
    h = h_ref[...]
    shift = mods_ref[mod_row, seq_row, :]
    ha_ref[...] = h * (gn_ref[...] * (1.0 + mods_ref[mod_row + 1, seq_row, :]))
    r = lax.rsqrt(jnp.mean(h * h, axis=-1, keepdims=True) + EPS)
    ha = ha_ref[...]
    g = r * _dot(ha, wg_ref[:, c0]) + sproj_ref[seq_row, :FF_CHUNK]
    u = r * _dot(ha, wu_ref[:, c0]) + sproj_ref[seq_row, FF_CHUNK:]
    act_ref[:, c0] = jax.nn.silu(g) * u
    xn_ref[...] = ha_ref[...] * r + shift
    for c in range(1, D_FF // FF_CHUNK):
        cs = slice(c * FF_CHUNK, (c + 1) * FF_CHUNK)
        xn = xn_ref[...]
        g = _dot(xn, wg_ref[:, cs])
        u = _dot(xn, wu_ref[:, cs])
        act_ref[:, cs] = jax.nn.silu(g) * u
    ff = _dot(act_ref[...], wd_ref[...])
    gate = mods_ref[mod_row + 2, seq_row, :]
    o_ref[...] = h_ref[...] + 0.5 * gate * ff


def _ffn(h2d, mods, g_norm, w_gate, w_up, w_down, *, mod_row, seq, cast=()):
    n_tok = h2d.shape[0]
    n_mod_rows, bsz, _ = mods.shape
    tm = TM_FFN_BF16_WEIGHTS if w_gate.dtype == BF16 else TM_FFN_F32_WEIGHTS
    assert seq % tm == 0
    n_steps = n_tok // tm
    tiles_per_seq = seq // tm
    row_spec = pl.BlockSpec((tm, D_MODEL), lambda i: (i, 0))
    cast_specs, cast_shapes = _cast_rider_specs(cast, n_steps)
    in_specs = [row_spec,
                _const_spec((n_mod_rows, bsz, D_MODEL)),
                _const_spec((1, D_MODEL)),
                _const_spec(w_gate.shape),
                _const_spec(w_up.shape),
                _const_spec(w_down.shape)] + cast_specs
    out_specs = [row_spec] + cast_specs
    out_shapes = [jax.ShapeDtypeStruct((n_tok, D_MODEL), F32)] + cast_shapes
    operands = [h2d, mods, g_norm, w_gate, w_up, w_down, *cast]
    return pl.pallas_call(
        functools.partial(_ffn_kernel, mod_row=mod_row, tiles_per_seq=tiles_per_seq,
                          n_cast=len(cast)),
        out_shape=out_shapes,
        grid=(n_steps,),
        in_specs=in_specs,
        out_specs=out_specs,
        scratch_shapes=[pltpu.VMEM((tm, D_MODEL), F32),
                        pltpu.VMEM((tm, D_MODEL), F32),
                        pltpu.VMEM((tm, D_FF), F32),
                        pltpu.VMEM((bsz, 2 * FF_CHUNK), F32)],
        compiler_params=pltpu.CompilerParams(
            dimension_semantics=("arbitrary",), vmem_limit_bytes=VMEM_LIMIT_FFN),
        name=f"ffn_mod{mod_row}",
    )(*operands)


def _head_rms(x, gain):
    lane = lax.broadcasted_iota(jnp.int32, (x.shape[0], LANES), 1)
    left = lane < HEAD_DIM
    cols = []
    for t in range(x.shape[1] // LANES):
        xt = x[:, t * LANES:(t + 1) * LANES]
        sq = xt * xt
        s_all = jnp.sum(sq, axis=-1, keepdims=True)
        s_left = jnp.sum(jnp.where(left, sq, 0.0), axis=-1, keepdims=True)
        ms = jnp.where(left, s_left, s_all - s_left) * (1.0 / HEAD_DIM)
        cols.append(xt * lax.rsqrt(ms + EPS))
    return jnp.concatenate(cols, axis=1) * gain


def _pair_blocks(x):
    lane = lax.broadcasted_iota(jnp.int32, x.shape, 1)
    left = lane < HEAD_DIM
    xr = pltpu.roll(x, HEAD_DIM, axis=1)
    blk0 = jnp.concatenate([jnp.where(left, x, 0.0), jnp.where(left, 0.0, xr)], axis=0)
    blk1 = jnp.concatenate([jnp.where(left, xr, 0.0), jnp.where(left, 0.0, x)], axis=0)
    return blk0.astype(BF16), blk1.astype(BF16)


def _mixer_kernel(*refs, mod_row, tiles_per_seq, n_cast):
    refs = list(refs)
    (sink_ref, h_ref, mods_ref, gn_ref, win_ref, gln_ref, bln_ref, ws_ref, bs_ref, gq_ref, gk_ref,
     wa_ref, wb_ref, wo_ref) = refs[:14]
    del refs[:14]
    cast_in = refs[:n_cast]
    del refs[:n_cast]
    o_ref = refs.pop(0)
    cast_out = refs[:n_cast]
    del refs[:n_cast]
    (xn_ref, uv_ref, vn_ref, ya_ref, qkv_ref, qn_ref, attn_ref, k_ref, v_ref,
     sg_ref, mg_ref, ha_ref, sproj_ref) = refs
    tm = h_ref.shape[0]
    n_blk = tm // CHUNK
    first_tile = (pl.program_id(0) % tiles_per_seq) == 0
    blocks = [slice(b * CHUNK, (b + 1) * CHUNK) for b in range(n_blk)]

    @pl.when(first_tile)
    def _():
        k_ref[0:CHUNK, :] = jnp.zeros((CHUNK, D_KV), F32)
        v_ref[0:CHUNK, :] = jnp.zeros((CHUNK, D_KV), F32)

    qkv_cols = slice(COL_QKV, COL_QKV + D_B + 2 * D_KV)

    @pl.when(pl.program_id(0) == 0)
    def _():
        sproj_ref[...] = _dot(mods_ref[mod_row].astype(BF16), win_ref[:, qkv_cols])

    _run_cast_riders(cast_in, cast_out)

    seq_row = pl.ds(pl.program_id(0) // tiles_per_seq, 1)
    h = h_ref[...]
    ha = h * (gn_ref[...] * (1.0 + mods_ref[mod_row + 1, seq_row, :]))
    ha_ref[...] = ha.astype(BF16)
    r = lax.rsqrt(jnp.mean(h * h, axis=-1, keepdims=True) + EPS)
    qkv_ref[...] = r * jnp.dot(ha_ref[...], win_ref[:, qkv_cols],
                               preferred_element_type=F32) + sproj_ref[seq_row, :]
    xn_ref[...] = (ha * r + mods_ref[mod_row, seq_row, :]).astype(BF16)

    n_gate_slices = 2 * n_blk
    gate_w = 2 * D_MODEL // n_gate_slices
    gate_slices = iter(range(n_gate_slices))

    def gate_slice():
        c0 = next(gate_slices) * gate_w
        sg_ref[:, c0:c0 + gate_w] = jax.nn.sigmoid(jnp.dot(
            xn_ref[...], win_ref[:, COL_GATES + c0:COL_GATES + c0 + gate_w],
            preferred_element_type=F32))

    uv_ref[...] = jnp.dot(xn_ref[...], win_ref[:, COL_UV:COL_UV + 2 * D_A],
                          preferred_element_type=F32)

    def mixer_a_block(rs):
        v = jax.nn.gelu(uv_ref[rs, D_A:])
        mu = jnp.mean(v, axis=-1, keepdims=True)
        var = jnp.mean(jnp.square(v - mu), axis=-1, keepdims=True)
        vn_ref[rs, :] = ((v - mu) * lax.rsqrt(var + EPS) * gln_ref[...] + bln_ref[...]).astype(BF16)
        uv_ref[rs, :D_A] = jax.nn.gelu(uv_ref[rs, :D_A])

    def spatial_gate():
        t_idx = lax.broadcasted_iota(jnp.int32, (CHUNK, CHUNK), 0)
        s_idx = lax.broadcasted_iota(jnp.int32, (CHUNK, CHUNK), 1)
        causal = s_idx <= t_idx
        for g in range(SGU_GROUPS):
            gs = slice(g * SGU_GROUP_DIM, (g + 1) * SGU_GROUP_DIM)
            w_g = jnp.where(causal, ws_ref[g], 0.0).astype(BF16)
            v_cat = jnp.concatenate([vn_ref[rs, gs] for rs in blocks], axis=1)
            z = jnp.dot(w_g, v_cat, preferred_element_type=F32)
            for b, rs in enumerate(blocks):
                zb = z[:, b * SGU_GROUP_DIM:(b + 1) * SGU_GROUP_DIM] + bs_ref[:, gs]
                ya_ref[rs, gs] = (uv_ref[rs, gs] * zb).astype(BF16)

    for b, rs in enumerate(blocks):
        qn_ref[rs, :] = (_head_rms(qkv_ref[rs, :D_B], gq_ref[...])
                         * (HEAD_DIM ** -0.5)).astype(BF16)
        k_ref[(b + 1) * CHUNK:(b + 2) * CHUNK, :] = _head_rms(
            qkv_ref[rs, D_B:D_B + D_KV], gk_ref[...])
        v_ref[(b + 1) * CHUNK:(b + 2) * CHUNK, :] = qkv_ref[rs, D_B + D_KV:]

    own_key = (lax.broadcasted_iota(jnp.int32, (CHUNK, CHUNK), 1)
               <= lax.broadcasted_iota(jnp.int32, (CHUNK, CHUNK), 0))
    prev_cap = jnp.where(first_tile, NEG, jnp.finfo(F32).max)
    lane = lax.broadcasted_iota(jnp.int32, (CHUNK, LANES), 1)
    left = lane < HEAD_DIM
    e_head = lax.broadcasted_iota(jnp.int32, (4 * CHUNK, LANES), 0) // (2 * CHUNK)
    e_lane = lax.broadcasted_iota(jnp.int32, (4 * CHUNK, LANES), 1) // HEAD_DIM
    ones_cols = jnp.where(e_head == e_lane, 1.0, 0.0).astype(BF16)
    tiles_per_kv = (N_Q_HEADS // N_KV_HEADS) // HEADS_PER_VREG

    def attention_block(b, rs):
        kblk = _pair_blocks(k_ref[b * CHUNK:(b + 2) * CHUNK, :])
        vblk = _pair_blocks(v_ref[b * CHUNK:(b + 2) * CHUNK, :])
        for j in range(N_KV_HEADS):
            tiles = [j * tiles_per_kv + t for t in range(tiles_per_kv)]
            q_rows = jnp.concatenate([qn_ref[rs, t * LANES:(t + 1) * LANES] for t in tiles], axis=0)
            s = lax.dot_general(q_rows, kblk[j], (((1,), (1,)), ((), ())),
                                preferred_element_type=F32)
            probs, sink_terms = [], []
            for i, t in enumerate(tiles):
                p_blocks, e_sinks = [], []
                for hh in range(HEADS_PER_VREG):
                    sink = sink_ref[t * HEADS_PER_VREG + hh]
                    col = hh * 2 * CHUNK
                    s_prev = s[i * CHUNK:(i + 1) * CHUNK, col:col + CHUNK]
                    s_own = s[i * CHUNK:(i + 1) * CHUNK, col + CHUNK:col + 2 * CHUNK]
                    if b == 0:
                        s_prev = jnp.minimum(s_prev, prev_cap)
                    folded = jnp.where(own_key, s_own, s_prev)
                    m = jnp.maximum(jnp.max(folded, axis=-1, keepdims=True), sink)
                    e = jnp.exp(folded - m)
                    p_blocks.append(jnp.where(own_key, 0.0, e).astype(BF16))
                    p_blocks.append(jnp.where(own_key, e, 0.0).astype(BF16))
                    e_sinks.append(jnp.exp(sink - m))
                probs.append(jnp.concatenate(p_blocks, axis=1))
                sink_terms.append(jnp.where(left, e_sinks[0], e_sinks[1]))
            pv = jnp.dot(jnp.concatenate(probs, axis=0),
                         jnp.concatenate([vblk[j], ones_cols], axis=1),
                         preferred_element_type=F32)
            out = pv[:, :LANES] / (pv[:, LANES:] + jnp.concatenate(sink_terms, axis=0))
            for i, t in enumerate(tiles):
                attn_ref[rs, t * LANES:(t + 1) * LANES] = out[i * CHUNK:(i + 1) * CHUNK].astype(BF16)

    for b, rs in enumerate(blocks):
        mixer_a_block(rs)
        gate_slice()
        attention_block(b, rs)
        gate_slice()
    spatial_gate()

    k_ref[0:CHUNK, :] = k_ref[tm:tm + CHUNK, :]
    v_ref[0:CHUNK, :] = v_ref[tm:tm + CHUNK, :]

    for c in range(D_MODEL // MXU_DIM):
        cs = slice(c * MXU_DIM, (c + 1) * MXU_DIM)
        cs_b = slice(D_MODEL + c * MXU_DIM, D_MODEL + (c + 1) * MXU_DIM)
        y_a = jnp.dot(ya_ref[...], wa_ref[:, cs], preferred_element_type=F32)
        y_b = jnp.dot(attn_ref[...], wb_ref[:, cs], preferred_element_type=F32)
        mg_ref[:, cs] = (sg_ref[:, cs] * y_a + sg_ref[:, cs_b] * y_b).astype(BF16)
    out = jnp.dot(mg_ref[...], wo_ref[...], preferred_element_type=F32)
    o_ref[...] = h_ref[...] + mods_ref[mod_row + 2, seq_row, :] * out


def _mixer(h2d, mods, g_norm, w_in, g_ln, b_ln, w_spatial, bias_rows, g_q, g_k, sinks,
           w_a, w_b, w_out, *, mod_row, seq, cast=()):
    n_tok = h2d.shape[0]
    n_mod_rows, bsz, _ = mods.shape
    n_steps = n_tok // TM_MIX
    tiles_per_seq = seq // TM_MIX
    row_spec = pl.BlockSpec((TM_MIX, D_MODEL), lambda i: (i, 0))
    cast_specs, cast_shapes = _cast_rider_specs(cast, n_steps)
    return pl.pallas_call(
        functools.partial(_mixer_kernel, mod_row=mod_row, tiles_per_seq=tiles_per_seq,
                          n_cast=len(cast)),
        out_shape=[jax.ShapeDtypeStruct((n_tok, D_MODEL), F32)] + cast_shapes,
        grid=(n_steps,),
        in_specs=[pl.BlockSpec(memory_space=pltpu.SMEM),
                  row_spec,
                  _const_spec((n_mod_rows, bsz, D_MODEL)),
                  _const_spec((1, D_MODEL)),
                  _const_spec((D_MODEL, IN_COLS)),
                  _const_spec((1, D_A)),
                  _const_spec((1, D_A)),
                  _const_spec((SGU_GROUPS, CHUNK, CHUNK)),
                  _const_spec((CHUNK, D_A)),
                  _const_spec((1, D_B)),
                  _const_spec((1, D_KV)),
                  _const_spec((D_A, D_MODEL)),
                  _const_spec((D_B, D_MODEL)),
                  _const_spec((D_MODEL, D_MODEL))] + cast_specs,
        out_specs=[row_spec] + cast_specs,
        scratch_shapes=[pltpu.VMEM((TM_MIX, D_MODEL), BF16),
                        pltpu.VMEM((TM_MIX, 2 * D_A), F32),
                        pltpu.VMEM((TM_MIX, D_A), BF16),
                        pltpu.VMEM((TM_MIX, D_A), BF16),
                        pltpu.VMEM((TM_MIX, D_B + 2 * D_KV), F32),
                        pltpu.VMEM((TM_MIX, D_B), BF16),
                        pltpu.VMEM((TM_MIX, D_B), BF16),
                        pltpu.VMEM((TM_MIX + CHUNK, D_KV), F32),
                        pltpu.VMEM((TM_MIX + CHUNK, D_KV), F32),
                        pltpu.VMEM((TM_MIX, 2 * D_MODEL), F32),
                        pltpu.VMEM((TM_MIX, D_MODEL), BF16),
                        pltpu.VMEM((TM_MIX, D_MODEL), BF16),
                        pltpu.VMEM((bsz, D_B + 2 * D_KV), F32)],
        compiler_params=pltpu.CompilerParams(
            dimension_semantics=("arbitrary",), vmem_limit_bytes=VMEM_LIMIT_MIX),
        name="mixer",
    )(sinks, h2d, mods, g_norm, w_in, g_ln, b_ln, w_spatial, bias_rows, g_q, g_k,
      w_a, w_b, w_out, *cast)


def kernel(x, c, w_ada, b_ada, g_norm1, ffn1_w_gate, ffn1_w_up, ffn1_w_down, g_norm2, w_in,
           g_sgu_ln, b_sgu_ln, w_spatial, b_spatial, g_q, g_k, attn_sinks, w_branch_a,
           w_branch_b, w_out, g_norm3, ffn2_w_gate, ffn2_w_up, ffn2_w_down):
    bsz, seq, d = x.shape
    assert d == D_MODEL and seq % TM_MIX == 0
    assert w_ada.shape[0] == 1, "single-layer problem"
    h = x.reshape(bsz * seq, d)
    b_ada2 = b_ada[0].reshape(1, N_MOD * d)

    mods = _adaln_mods(c, w_ada[0], b_ada2, N_MOD)

    h, w_in_b, w_a_b, w_b_b, w_out_b = _ffn(
        h, mods, g_norm1, ffn1_w_gate[0], ffn1_w_up[0], ffn1_w_down[0], mod_row=0, seq=seq,
        cast=(w_in[0], w_branch_a[0], w_branch_b[0], w_out[0]))

    bias_rows = jnp.repeat(b_spatial[0].T, SGU_GROUP_DIM, axis=1)
    h, w_gate_b, w_up_b, w_down_b = _mixer(
        h, mods, g_norm2, w_in_b, g_sgu_ln, b_sgu_ln, w_spatial[0], bias_rows,
        jnp.tile(g_q[0], N_Q_HEADS)[None], jnp.tile(g_k[0], N_KV_HEADS)[None], attn_sinks[0],
        w_a_b, w_b_b, w_out_b, mod_row=3, seq=seq,
        cast=(ffn2_w_gate[0], ffn2_w_up[0], ffn2_w_down[0]))

    (h,) = _ffn(h, mods, g_norm3, w_gate_b, w_up_b, w_down_b, mod_row=6, seq=seq)
    return h.reshape(bsz, seq, d)
```

```python
import functools

import jax
import jax.numpy as jnp
from jax import lax
from jax.experimental import pallas as pl
from jax.experimental.pallas import tpu as pltpu

F32 = jnp.float32
BF16 = jnp.bfloat16

D_MODEL = 1024
D_FF = 2816
CHUNK = 128
D_A = 512
SGU_GROUPS = 4
SGU_GROUP_DIM = D_A // SGU_GROUPS
HEAD_DIM = 64
N_Q_HEADS = 8
N_KV_HEADS = 2
D_B = N_Q_HEADS * HEAD_DIM
D_KV = N_KV_HEADS * HEAD_DIM
N_MOD = 9
EPS = 1e-6
NEG = -1e30

COL_UV = 0
COL_QKV = 2 * D_A
COL_GATES = COL_QKV + D_B + 2 * D_KV
IN_COLS = COL_GATES + 2 * D_MODEL

LANES = 128
MXU_DIM = 256
HEADS_PER_VREG = LANES // HEAD_DIM
BF16_SUBLANES = 16

TM_FFN_F32_WEIGHTS = 512
TM_FFN_BF16_WEIGHTS = 1024
TM_MIX = 1024
SUB_MIX = 512
FF_CHUNK = MXU_DIM
VMEM_LIMIT_FFN = 58 * 1024 * 1024
VMEM_LIMIT_MIX = 52 * 1024 * 1024


def _const_spec(shape):
    zeros = (0,) * len(shape)
    return pl.BlockSpec(shape, lambda i: zeros, pipeline_mode=pl.Buffered(1))


def _dot(x, w):
    return lax.dot_general(x, w, (((1,), (0,)), ((), ())), preferred_element_type=F32)


def _cast_rider_specs(weights, n_steps):
    specs, shapes = [], []
    for w in weights:
        n_rows, n_cols = w.shape
        rows = BF16_SUBLANES
        while n_rows % rows or n_rows // rows > n_steps:
            rows += BF16_SUBLANES
        last = n_rows // rows - 1
        specs.append(pl.BlockSpec((rows, n_cols), lambda i, last=last: (jnp.minimum(i, last), 0)))
        shapes.append(jax.ShapeDtypeStruct(w.shape, BF16))
    return specs, shapes


def _run_cast_riders(in_refs, out_refs):
    for src, dst in zip(in_refs, out_refs):
        dst[...] = src[...].astype(BF16)


def _mods_kernel(c_ref, w_ref, b_ref, o_ref):
    o_ref[0] = _dot(jax.nn.silu(c_ref[...]), w_ref[...]) + b_ref[...]


def _adaln_mods(c, w_ada, b_ada2, n_rows):
    bsz = c.shape[0]
    return pl.pallas_call(
        _mods_kernel,
        out_shape=jax.ShapeDtypeStruct((n_rows, bsz, D_MODEL), F32),
        grid=(n_rows,),
        in_specs=[pl.BlockSpec((bsz, D_MODEL), lambda j: (0, 0)),
                  pl.BlockSpec((D_MODEL, D_MODEL), lambda j: (0, j)),
                  pl.BlockSpec((1, D_MODEL), lambda j: (0, j))],
        out_specs=pl.BlockSpec((1, bsz, D_MODEL), lambda j: (j, 0, 0)),
        compiler_params=pltpu.CompilerParams(dimension_semantics=("arbitrary",)),
        name="adaln_mods",
    )(c, w_ada, b_ada2)


def _ffn_kernel(*refs, mod_row, tiles_per_seq, n_cast):
    refs = list(refs)
    h_ref, mods_ref, gn_ref, wg_ref, wu_ref, wd_ref = refs[:6]
    del refs[:6]
    cast_in = refs[:n_cast]
    del refs[:n_cast]
    o_ref = refs.pop(0)
    cast_out = refs[:n_cast]
    del refs[:n_cast]
    xn_ref, ha_ref, act_ref, sproj_ref = refs

    step = pl.program_id(0)
    c0 = slice(0, FF_CHUNK)

    @pl.when(step == 0)
    def _():
        sproj_ref[:, :FF_CHUNK] = _dot(mods_ref[mod_row], wg_ref[:, c0])
        sproj_ref[:, FF_CHUNK:] = _dot(mods_ref[mod_row], wu_ref[:, c0])

    _run_cast_riders(cast_in, cast_out)

    seq_row = pl.ds(step // tiles_per_seq, 1)
    h = h_ref[...]
    shift = mods_ref[mod_row, seq_row, :]
    ha_ref[...] = h * (gn_ref[...] * (1.0 + mods_ref[mod_row + 1, seq_row, :]))
    r = lax.rsqrt(jnp.mean(h * h, axis=-1, keepdims=True) + EPS)
    ha = ha_ref[...]
    g = r * _dot(ha, wg_ref[:, c0]) + sproj_ref[seq_row, :FF_CHUNK]
    u = r * _dot(ha, wu_ref[:, c0]) + sproj_ref[seq_row, FF_CHUNK:]
    act_ref[:, c0] = jax.nn.silu(g) * u
    xn_ref[...] = ha_ref[...] * r + shift
    for c in range(1, D_FF // FF_CHUNK):
        cs = slice(c * FF_CHUNK, (c + 1) * FF_CHUNK)
        xn = xn_ref[...]
        g = _dot(xn, wg_ref[:, cs])
        u = _dot(xn, wu_ref[:, cs])
        act_ref[:, cs] = jax.nn.silu(g) * u
    ff = _dot(act_ref[...], wd_ref[...])
    gate = mods_ref[mod_row + 2, seq_row, :]
    o_ref[...] = h_ref[...] + 0.5 * gate * ff


def _ffn(h2d, mods, g_norm, w_gate, w_up, w_down, *, mod_row, seq, cast=()):
    n_tok = h2d.shape[0]
    n_mod_rows, bsz, _ = mods.shape
    tm = TM_FFN_BF16_WEIGHTS if w_gate.dtype == BF16 else TM_FFN_F32_WEIGHTS
    assert seq % tm == 0
    n_steps = n_tok // tm
    tiles_per_seq = seq // tm
    row_spec = pl.BlockSpec((tm, D_MODEL), lambda i: (i, 0))
    cast_specs, cast_shapes = _cast_rider_specs(cast, n_steps)
    in_specs = [row_spec,
                _const_spec((n_mod_rows, bsz, D_MODEL)),
                _const_spec((1, D_MODEL)),
                _const_spec(w_gate.shape),
                _const_spec(w_up.shape),
                _const_spec(w_down.shape)] + cast_specs
    out_specs = [row_spec] + cast_specs
    out_shapes = [jax.ShapeDtypeStruct((n_tok, D_MODEL), F32)] + cast_shapes
    operands = [h2d, mods, g_norm, w_gate, w_up, w_down, *cast]
    return pl.pallas_call(
        functools.partial(_ffn_kernel, mod_row=mod_row, tiles_per_seq=tiles_per_seq,
                          n_cast=len(cast)),
        out_shape=out_shapes,
        grid=(n_steps,),
        in_specs=in_specs,
        out_specs=out_specs,
        scratch_shapes=[pltpu.VMEM((tm, D_MODEL), F32),
                        pltpu.VMEM((tm, D_MODEL), F32),
                        pltpu.VMEM((tm, D_FF), F32),
                        pltpu.VMEM((bsz, 2 * FF_CHUNK), F32)],
        compiler_params=pltpu.CompilerParams(
            dimension_semantics=("arbitrary",), vmem_limit_bytes=VMEM_LIMIT_FFN),
        name=f"ffn_mod{mod_row}",
    )(*operands)


def _head_rms(x, gain):
    lane = lax.broadcasted_iota(jnp.int32, (x.shape[0], LANES), 1)
    left = lane < HEAD_DIM
    cols = []
    for t in range(x.shape[1] // LANES):
        xt = x[:, t * LANES:(t + 1) * LANES]
        sq = xt * xt
        s_all = jnp.sum(sq, axis=-1, keepdims=True)
        s_left = jnp.sum(jnp.where(left, sq, 0.0), axis=-1, keepdims=True)
        ms = jnp.where(left, s_left, s_all - s_left) * (1.0 / HEAD_DIM)
        cols.append(xt * lax.rsqrt(ms + EPS))
    return jnp.concatenate(cols, axis=1) * gain


def _pair_blocks(x):
    lane = lax.broadcasted_iota(jnp.int32, x.shape, 1)
    left = lane < HEAD_DIM
    xr = pltpu.roll(x, HEAD_DIM, axis=1)
    blk0 = jnp.concatenate([jnp.where(left, x, 0.0), jnp.where(left, 0.0, xr)], axis=0)
    blk1 = jnp.concatenate([jnp.where(left, xr, 0.0), jnp.where(left, 0.0, x)], axis=0)
    return blk0.astype(BF16), blk1.astype(BF16)


def _mixer_kernel(*refs, mod_row, tiles_per_seq, n_cast):
    refs = list(refs)
    (sink_ref, h_ref, mods_ref, gn_ref, win_ref, gln_ref, bln_ref, ws_ref, bs_ref, gq_ref, gk_ref,
     wa_ref, wb_ref, wo_ref) = refs[:14]
    del refs[:14]
    cast_in = refs[:n_cast]
    del refs[:n_cast]
    o_ref = refs.pop(0)
    cast_out = refs[:n_cast]
    del refs[:n_cast]
    (xn_ref, uv_ref, vn_ref, ya_ref, qkv_ref, qn_ref, attn_ref, k_ref, v_ref,
     sg_ref, mg_ref, ha_ref, sproj_ref) = refs
    tm = h_ref.shape[0]
    first_tile = (pl.program_id(0) % tiles_per_seq) == 0

    @pl.when(first_tile)
    def _():
        k_ref[0:CHUNK, :] = jnp.zeros((CHUNK, D_KV), F32)
        v_ref[0:CHUNK, :] = jnp.zeros((CHUNK, D_KV), F32)

    qkv_cols = slice(COL_QKV, COL_QKV + D_B + 2 * D_KV)

    @pl.when(pl.program_id(0) == 0)
    def _():
        sproj_ref[...] = _dot(mods_ref[mod_row].astype(BF16), win_ref[:, qkv_cols])

    _run_cast_riders(cast_in, cast_out)
    seq_row = pl.ds(pl.program_id(0) // tiles_per_seq, 1)

    own_key = (lax.broadcasted_iota(jnp.int32, (CHUNK, CHUNK), 1)
               <= lax.broadcasted_iota(jnp.int32, (CHUNK, CHUNK), 0))
    prev_cap = jnp.where(first_tile, NEG, jnp.finfo(F32).max)
    lane = lax.broadcasted_iota(jnp.int32, (CHUNK, LANES), 1)
    left = lane < HEAD_DIM
    e_head = lax.broadcasted_iota(jnp.int32, (4 * CHUNK, LANES), 0) // (2 * CHUNK)
    e_lane = lax.broadcasted_iota(jnp.int32, (4 * CHUNK, LANES), 1) // HEAD_DIM
    ones_cols = jnp.where(e_head == e_lane, 1.0, 0.0).astype(BF16)
    tiles_per_kv = (N_Q_HEADS // N_KV_HEADS) // HEADS_PER_VREG

    def sub_tile(r0):
        rows = slice(r0, r0 + SUB_MIX)
        n_blk = SUB_MIX // CHUNK
        b0 = r0 // CHUNK
        blocks = [slice(r0 + b * CHUNK, r0 + (b + 1) * CHUNK) for b in range(n_blk)]

        h = h_ref[rows, :]
        ha = h * (gn_ref[...] * (1.0 + mods_ref[mod_row + 1, seq_row, :]))
        ha_ref[rows, :] = ha.astype(BF16)
        r = lax.rsqrt(jnp.mean(h * h, axis=-1, keepdims=True) + EPS)
        qkv_ref[rows, :] = r * jnp.dot(ha_ref[rows, :], win_ref[:, qkv_cols],
                                       preferred_element_type=F32) + sproj_ref[seq_row, :]
        xn_ref[rows, :] = (ha * r + mods_ref[mod_row, seq_row, :]).astype(BF16)

        n_gate_slices = 2 * n_blk
        gate_w = 2 * D_MODEL // n_gate_slices
        gate_slices = iter(range(n_gate_slices))

        def gate_slice():
            c0 = next(gate_slices) * gate_w
            sg_ref[rows, c0:c0 + gate_w] = jax.nn.sigmoid(jnp.dot(
                xn_ref[rows, :], win_ref[:, COL_GATES + c0:COL_GATES + c0 + gate_w],
                preferred_element_type=F32))

        uv_ref[rows, :] = jnp.dot(xn_ref[rows, :], win_ref[:, COL_UV:COL_UV + 2 * D_A],
                                  preferred_element_type=F32)

        def mixer_a_block(rs):
            v = jax.nn.gelu(uv_ref[rs, D_A:])
            mu = jnp.mean(v, axis=-1, keepdims=True)
            var = jnp.mean(jnp.square(v - mu), axis=-1, keepdims=True)
            vn_ref[rs, :] = ((v - mu) * lax.rsqrt(var + EPS) * gln_ref[...] + bln_ref[...]).astype(BF16)
            uv_ref[rs, :D_A] = jax.nn.gelu(uv_ref[rs, :D_A])

        def spatial_gate():
            t_idx = lax.broadcasted_iota(jnp.int32, (CHUNK, CHUNK), 0)
            s_idx = lax.broadcasted_iota(jnp.int32, (CHUNK, CHUNK), 1)
            causal = s_idx <= t_idx
            for g in range(SGU_GROUPS):
                gs = slice(g * SGU_GROUP_DIM, (g + 1) * SGU_GROUP_DIM)
                w_g = jnp.where(causal, ws_ref[g], 0.0).astype(BF16)
                v_cat = jnp.concatenate([vn_ref[rs, gs] for rs in blocks], axis=1)
                z = jnp.dot(w_g, v_cat, preferred_element_type=F32)
                for b, rs in enumerate(blocks):
                    zb = z[:, b * SGU_GROUP_DIM:(b + 1) * SGU_GROUP_DIM] + bs_ref[:, gs]
                    ya_ref[rs, gs] = (uv_ref[rs, gs] * zb).astype(BF16)

        for b, rs in enumerate(blocks):
            qn_ref[rs, :] = (_head_rms(qkv_ref[rs, :D_B], gq_ref[...])
                             * (HEAD_DIM ** -0.5)).astype(BF16)
            k_ref[(b0 + b + 1) * CHUNK:(b0 + b + 2) * CHUNK, :] = _head_rms(
                qkv_ref[rs, D_B:D_B + D_KV], gk_ref[...])
            v_ref[(b0 + b + 1) * CHUNK:(b0 + b + 2) * CHUNK, :] = qkv_ref[rs, D_B + D_KV:]

        def attention_block(b, rs):
            kblk = _pair_blocks(k_ref[(b0 + b) * CHUNK:(b0 + b + 2) * CHUNK, :])
            vblk = _pair_blocks(v_ref[(b0 + b) * CHUNK:(b0 + b + 2) * CHUNK, :])
            for j in range(N_KV_HEADS):
                tiles = [j * tiles_per_kv + t for t in range(tiles_per_kv)]
                q_rows = jnp.concatenate([qn_ref[rs, t * LANES:(t + 1) * LANES] for t in tiles], axis=0)
                s = lax.dot_general(q_rows, kblk[j], (((1,), (1,)), ((), ())),
                                    preferred_element_type=F32)
                probs, sink_terms = [], []
                for i, t in enumerate(tiles):
                    p_blocks, e_sinks = [], []
                    for hh in range(HEADS_PER_VREG):
                        sink = sink_ref[t * HEADS_PER_VREG + hh]
                        col = hh * 2 * CHUNK
                        s_prev = s[i * CHUNK:(i + 1) * CHUNK, col:col + CHUNK]
                        s_own = s[i * CHUNK:(i + 1) * CHUNK, col + CHUNK:col + 2 * CHUNK]
                        if b0 + b == 0:
                            s_prev = jnp.minimum(s_prev, prev_cap)
                        folded = jnp.where(own_key, s_own, s_prev)
                        m = jnp.maximum(jnp.max(folded, axis=-1, keepdims=True), sink)
                        e = jnp.exp(folded - m)
                        p_blocks.append(jnp.where(own_key, 0.0, e).astype(BF16))
                        p_blocks.append(jnp.where(own_key, e, 0.0).astype(BF16))
                        e_sinks.append(jnp.exp(sink - m))
                    probs.append(jnp.concatenate(p_blocks, axis=1))
                    sink_terms.append(jnp.where(left, e_sinks[0], e_sinks[1]))
                pv = jnp.dot(jnp.concatenate(probs, axis=0),
                             jnp.concatenate([vblk[j], ones_cols], axis=1),
                             preferred_element_type=F32)
                out = pv[:, :LANES] / (pv[:, LANES:] + jnp.concatenate(sink_terms, axis=0))
                for i, t in enumerate(tiles):
                    attn_ref[rs, t * LANES:(t + 1) * LANES] = out[i * CHUNK:(i + 1) * CHUNK].astype(BF16)

        for b, rs in enumerate(blocks):
            mixer_a_block(rs)
            gate_slice()
            attention_block(b, rs)
            gate_slice()
        spatial_gate()

        for c in range(D_MODEL // MXU_DIM):
            cs = slice(c * MXU_DIM, (c + 1) * MXU_DIM)
            cs_b = slice(D_MODEL + c * MXU_DIM, D_MODEL + (c + 1) * MXU_DIM)
            y_a = jnp.dot(ya_ref[rows, :], wa_ref[:, cs], preferred_element_type=F32)
            y_b = jnp.dot(attn_ref[rows, :], wb_ref[:, cs], preferred_element_type=F32)
            mg_ref[rows, cs] = (sg_ref[rows, cs] * y_a + sg_ref[rows, cs_b] * y_b).astype(BF16)
        out = jnp.dot(mg_ref[rows, :], wo_ref[...], preferred_element_type=F32)
        o_ref[rows, :] = h_ref[rows, :] + mods_ref[mod_row + 2, seq_row, :] * out

    for r0 in range(0, tm, SUB_MIX):
        sub_tile(r0)

    k_ref[0:CHUNK, :] = k_ref[tm:tm + CHUNK, :]
    v_ref[0:CHUNK, :] = v_ref[tm:tm + CHUNK, :]


def _mixer(h2d, mods, g_norm, w_in, g_ln, b_ln, w_spatial, bias_rows, g_q, g_k, sinks,
           w_a, w_b, w_out, *, mod_row, seq, cast=()):
    n_tok = h2d.shape[0]
    n_mod_rows, bsz, _ = mods.shape
    n_steps = n_tok // TM_MIX
    tiles_per_seq = seq // TM_MIX
    row_spec = pl.BlockSpec((TM_MIX, D_MODEL), lambda i: (i, 0))
    cast_specs, cast_shapes = _cast_rider_specs(cast, n_steps)
    return pl.pallas_call(
        functools.partial(_mixer_kernel, mod_row=mod_row, tiles_per_seq=tiles_per_seq,
                          n_cast=len(cast)),
        out_shape=[jax.ShapeDtypeStruct((n_tok, D_MODEL), F32)] + cast_shapes,
        grid=(n_steps,),
        in_specs=[pl.BlockSpec(memory_space=pltpu.SMEM),
                  row_spec,
                  _const_spec((n_mod_rows, bsz, D_MODEL)),
                  _const_spec((1, D_MODEL)),
                  _const_spec((D_MODEL, IN_COLS)),
                  _const_spec((1, D_A)),
                  _const_spec((1, D_A)),
                  _const_spec((SGU_GROUPS, CHUNK, CHUNK)),
                  _const_spec((CHUNK, D_A)),
                  _const_spec((1, D_B)),
                  _const_spec((1, D_KV)),
                  _const_spec((D_A, D_MODEL)),
                  _const_spec((D_B, D_MODEL)),
                  _const_spec((D_MODEL, D_MODEL))] + cast_specs,
        out_specs=[row_spec] + cast_specs,
        scratch_shapes=[pltpu.VMEM((TM_MIX, D_MODEL), BF16),
                        pltpu.VMEM((TM_MIX, 2 * D_A), F32),
                        pltpu.VMEM((TM_MIX, D_A), BF16),
                        pltpu.VMEM((TM_MIX, D_A), BF16),
                        pltpu.VMEM((TM_MIX, D_B + 2 * D_KV), F32),
                        pltpu.VMEM((TM_MIX, D_B), BF16),
                        pltpu.VMEM((TM_MIX, D_B), BF16),
                        pltpu.VMEM((TM_MIX + CHUNK, D_KV), F32),
                        pltpu.VMEM((TM_MIX + CHUNK, D_KV), F32),
                        pltpu.VMEM((TM_MIX, 2 * D_MODEL), F32),
                        pltpu.VMEM((TM_MIX, D_MODEL), BF16),
                        pltpu.VMEM((TM_MIX, D_MODEL), BF16),
                        pltpu.VMEM((bsz, D_B + 2 * D_KV), F32)],
        compiler_params=pltpu.CompilerParams(
            dimension_semantics=("arbitrary",), vmem_limit_bytes=VMEM_LIMIT_MIX),
        name="mixer",
    )(sinks, h2d, mods, g_norm, w_in, g_ln, b_ln, w_spatial, bias_rows, g_q, g_k,
      w_a, w_b, w_out, *cast)


def kernel(x, c, w_ada, b_ada, g_norm1, ffn1_w_gate, ffn1_w_up, ffn1_w_down, g_norm2, w_in,
           g_sgu_ln, b_sgu_ln, w_spatial, b_spatial, g_q, g_k, attn_sinks, w_branch_a,
           w_branch_b, w_out, g_norm3, ffn2_w_gate, ffn2_w_up, ffn2_w_down):
    bsz, seq, d = x.shape
    assert d == D_MODEL and seq % TM_MIX == 0
    assert w_ada.shape[0] == 1, "single-layer problem"
    h = x.reshape(bsz * seq, d)
    b_ada2 = b_ada[0].reshape(1, N_MOD * d)

    mods = _adaln_mods(c, w_ada[0], b_ada2, N_MOD)

    h, w_in_b, w_a_b, w_b_b, w_out_b = _ffn(
        h, mods, g_norm1, ffn1_w_gate[0], ffn1_w_up[0], ffn1_w_down[0], mod_row=0, seq=seq,
        cast=(w_in[0], w_branch_a[0], w_branch_b[0], w_out[0]))

    bias_rows = jnp.repeat(b_spatial[0].T, SGU_GROUP_DIM, axis=1)
    h, w_gate_b, w_up_b, w_down_b = _mixer(
        h, mods, g_norm2, w_in_b, g_sgu_ln, b_sgu_ln, w_spatial[0], bias_rows,
        jnp.tile(g_q[0], N_Q_HEADS)[None], jnp.tile(g_k[0], N_KV_HEADS)[None], attn_sinks[0],
        w_a_b, w_b_b, w_out_b, mod_row=3, seq=seq,
        cast=(ffn2_w_gate[0], ffn2_w_up[0], ffn2_w_down[0]))

    (h,) = _ffn(h, mods, g_norm3, w_gate_b, w_up_b, w_down_b, mod_row=6, seq=seq)
    return h.reshape(bsz, seq, d)
```

```python
import functools

import jax
import jax.numpy as jnp
from jax import lax
from jax.experimental import pallas as pl
from jax.experimental.pallas import tpu as pltpu

F32 = jnp.float32
BF16 = jnp.bfloat16

D_MODEL = 1024
D_FF = 2816
CHUNK = 128
D_A = 512
SGU_GROUPS = 4
SGU_GROUP_DIM = D_A // SGU_GROUPS
HEAD_DIM = 64
N_Q_HEADS = 8
N_KV_HEADS = 2
D_B = N_Q_HEADS * HEAD_DIM
D_KV = N_KV_HEADS * HEAD_DIM
N_MOD = 9
EPS = 1e-6
NEG = -1e30

COL_UV = 0
COL_QKV = 2 * D_A
COL_GATES = COL_QKV + D_B + 2 * D_KV
IN_COLS = COL_GATES + 2 * D_MODEL

LANES = 128
MXU_DIM = 256
HEADS_PER_VREG = LANES // HEAD_DIM
BF16_SUBLANES = 16

TM_FFN = 1024
SUB_FFN_F32_WEIGHTS = 512
SUB_FFN_BF16_WEIGHTS = 1024
TM_MIX = 1024
SUB_MIX = 512
FF_CHUNK = MXU_DIM
VMEM_LIMIT_FFN_F32_WEIGHTS = 61 * 1024 * 1024
VMEM_LIMIT_FFN_BF16_WEIGHTS = 52 * 1024 * 1024
VMEM_LIMIT_MIX = 52 * 1024 * 1024


def _const_spec(shape):
    zeros = (0,) * len(shape)
    return pl.BlockSpec(shape, lambda i: zeros, pipeline_mode=pl.Buffered(1))


def _dot(x, w):
    return lax.dot_general(x, w, (((1,), (0,)), ((), ())), preferred_element_type=F32)


def _cast_rider_specs(weights, n_steps):
    specs, shapes = [], []
    for w in weights:
        n_rows, n_cols = w.shape
        rows = BF16_SUBLANES
        while n_rows % rows or n_rows // rows > n_steps:
            rows += BF16_SUBLANES
        last = n_rows // rows - 1
        specs.append(pl.BlockSpec((rows, n_cols), lambda i, last=last: (jnp.minimum(i, last), 0)))
        shapes.append(jax.ShapeDtypeStruct(w.shape, BF16))
    return specs, shapes


def _run_cast_riders(in_refs, out_refs):
    for src, dst in zip(in_refs, out_refs):
        dst[...] = src[...].astype(BF16)


def _mods_kernel(c_ref, w_ref, b_ref, o_ref):
    o_ref[0] = _dot(jax.nn.silu(c_ref[...]), w_ref[...]) + b_ref[...]


def _adaln_mods(c, w_ada, b_ada2, n_rows):
    bsz = c.shape[0]
    return pl.pallas_call(
        _mods_kernel,
        out_shape=jax.ShapeDtypeStruct((n_rows, bsz, D_MODEL), F32),
        grid=(n_rows,),
        in_specs=[pl.BlockSpec((bsz, D_MODEL), lambda j: (0, 0)),
                  pl.BlockSpec((D_MODEL, D_MODEL), lambda j: (0, j)),
                  pl.BlockSpec((1, D_MODEL), lambda j: (0, j))],
        out_specs=pl.BlockSpec((1, bsz, D_MODEL), lambda j: (j, 0, 0)),
        compiler_params=pltpu.CompilerParams(dimension_semantics=("arbitrary",)),
        name="adaln_mods",
    )(c, w_ada, b_ada2)


def _ffn_kernel(*refs, mod_row, tiles_per_seq, n_cast):
    refs = list(refs)
    h_ref, mods_ref, gn_ref, wg_ref, wu_ref, wd_ref = refs[:6]
    del refs[:6]
    cast_in = refs[:n_cast]
    del refs[:n_cast]
    o_ref = refs.pop(0)
    cast_out = refs[:n_cast]
    del refs[:n_cast]
    xn_ref, ha_ref, act_ref, sproj_ref = refs

    step = pl.program_id(0)
    c0 = slice(0, FF_CHUNK)
    sub = xn_ref.shape[0]

    @pl.when(step == 0)
    def _():
        sproj_ref[:, :FF_CHUNK] = _dot(mods_ref[mod_row], wg_ref[:, c0])
        sproj_ref[:, FF_CHUNK:] = _dot(mods_ref[mod_row], wu_ref[:, c0])

    _run_cast_riders(cast_in, cast_out)
    seq_row = pl.ds(step // tiles_per_seq, 1)
    shift = mods_ref[mod_row, seq_row, :]
    gate = mods_ref[mod_row + 2, seq_row, :]

    for r0 in range(0, h_ref.shape[0], sub):
        rows = slice(r0, r0 + sub)
        h = h_ref[rows, :]
        ha_ref[...] = h * (gn_ref[...] * (1.0 + mods_ref[mod_row + 1, seq_row, :]))
        r = lax.rsqrt(jnp.mean(h * h, axis=-1, keepdims=True) + EPS)
        ha = ha_ref[...]
        g = r * _dot(ha, wg_ref[:, c0]) + sproj_ref[seq_row, :FF_CHUNK]
        u = r * _dot(ha, wu_ref[:, c0]) + sproj_ref[seq_row, FF_CHUNK:]
        act_ref[:, c0] = jax.nn.silu(g) * u
        xn_ref[...] = ha_ref[...] * r + shift
        for c in range(1, D_FF // FF_CHUNK):
            cs = slice(c * FF_CHUNK, (c + 1) * FF_CHUNK)
            xn = xn_ref[...]
            g = _dot(xn, wg_ref[:, cs])
            u = _dot(xn, wu_ref[:, cs])
            act_ref[:, cs] = jax.nn.silu(g) * u
        ff = _dot(act_ref[...], wd_ref[...])
        o_ref[rows, :] = h_ref[rows, :] + 0.5 * gate * ff


def _ffn(h2d, mods, g_norm, w_gate, w_up, w_down, *, mod_row, seq, cast=()):
    n_tok = h2d.shape[0]
    n_mod_rows, bsz, _ = mods.shape
    tm = TM_FFN
    bf16_weights = w_gate.dtype == BF16
    sub = SUB_FFN_BF16_WEIGHTS if bf16_weights else SUB_FFN_F32_WEIGHTS
    vmem_limit = VMEM_LIMIT_FFN_BF16_WEIGHTS if bf16_weights else VMEM_LIMIT_FFN_F32_WEIGHTS
    assert seq % tm == 0 and tm % sub == 0
    n_steps = n_tok // tm
    tiles_per_seq = seq // tm
    row_spec = pl.BlockSpec((tm, D_MODEL), lambda i: (i, 0))
    cast_specs, cast_shapes = _cast_rider_specs(cast, n_steps)
    in_specs = [row_spec,
                _const_spec((n_mod_rows, bsz, D_MODEL)),
                _const_spec((1, D_MODEL)),
                _const_spec(w_gate.shape),
                _const_spec(w_up.shape),
                _const_spec(w_down.shape)] + cast_specs
    out_specs = [row_spec] + cast_specs
    out_shapes = [jax.ShapeDtypeStruct((n_tok, D_MODEL), F32)] + cast_shapes
    operands = [h2d, mods, g_norm, w_gate, w_up, w_down, *cast]
    return pl.pallas_call(
        functools.partial(_ffn_kernel, mod_row=mod_row, tiles_per_seq=tiles_per_seq,
                          n_cast=len(cast)),
        out_shape=out_shapes,
        grid=(n_steps,),
        in_specs=in_specs,
        out_specs=out_specs,
        scratch_shapes=[pltpu.VMEM((sub, D_MODEL), F32),
                        pltpu.VMEM((sub, D_MODEL), F32),
                        pltpu.VMEM((sub, D_FF), F32),
                        pltpu.VMEM((bsz, 2 * FF_CHUNK), F32)],
        compiler_params=pltpu.CompilerParams(
            dimension_semantics=("arbitrary",), vmem_limit_bytes=vmem_limit),
        name=f"ffn_mod{mod_row}",
    )(*operands)


def _head_rms(x, gain):
    lane = lax.broadcasted_iota(jnp.int32, (x.shape[0], LANES), 1)
    left = lane < HEAD_DIM
    cols = []
    for t in range(x.shape[1] // LANES):
        xt = x[:, t * LANES:(t + 1) * LANES]
        sq = xt * xt
        s_all = jnp.sum(sq, axis=-1, keepdims=True)
        s_left = jnp.sum(jnp.where(left, sq, 0.0), axis=-1, keepdims=True)
        ms = jnp.where(left, s_left, s_all - s_left) * (1.0 / HEAD_DIM)
        cols.append(xt * lax.rsqrt(ms + EPS))
    return jnp.concatenate(cols, axis=1) * gain


def _pair_blocks(x):
    lane = lax.broadcasted_iota(jnp.int32, x.shape, 1)
    left = lane < HEAD_DIM
    xr = pltpu.roll(x, HEAD_DIM, axis=1)
    blk0 = jnp.concatenate([jnp.where(left, x, 0.0), jnp.where(left, 0.0, xr)], axis=0)
    blk1 = jnp.concatenate([jnp.where(left, xr, 0.0), jnp.where(left, 0.0, x)], axis=0)
    return blk0.astype(BF16), blk1.astype(BF16)


def _mixer_kernel(*refs, mod_row, tiles_per_seq, n_cast):
    refs = list(refs)
    (sink_ref, h_ref, mods_ref, gn_ref, win_ref, gln_ref, bln_ref, ws_ref, bs_ref, gq_ref, gk_ref,
     wa_ref, wb_ref, wo_ref) = refs[:14]
    del refs[:14]
    cast_in = refs[:n_cast]
    del refs[:n_cast]
    o_ref = refs.pop(0)
    cast_out = refs[:n_cast]
    del refs[:n_cast]
    (xn_ref, uv_ref, vn_ref, ya_ref, qkv_ref, qn_ref, attn_ref, k_ref, v_ref,
     sg_ref, mg_ref, ha_ref, sproj_ref) = refs
    tm = h_ref.shape[0]
    first_tile = (pl.program_id(0) % tiles_per_seq) == 0

    @pl.when(first_tile)
    def _():
        k_ref[0:CHUNK, :] = jnp.zeros((CHUNK, D_KV), F32)
        v_ref[0:CHUNK, :] = jnp.zeros((CHUNK, D_KV), F32)

    qkv_cols = slice(COL_QKV, COL_QKV + D_B + 2 * D_KV)

    @pl.when(pl.program_id(0) == 0)
    def _():
        sproj_ref[...] = _dot(mods_ref[mod_row].astype(BF16), win_ref[:, qkv_cols])

    _run_cast_riders(cast_in, cast_out)
    seq_row = pl.ds(pl.program_id(0) // tiles_per_seq, 1)

    own_key = (lax.broadcasted_iota(jnp.int32, (CHUNK, CHUNK), 1)
               <= lax.broadcasted_iota(jnp.int32, (CHUNK, CHUNK), 0))
    prev_cap = jnp.where(first_tile, NEG, jnp.finfo(F32).max)
    lane = lax.broadcasted_iota(jnp.int32, (CHUNK, LANES), 1)
    left = lane < HEAD_DIM
    e_head = lax.broadcasted_iota(jnp.int32, (4 * CHUNK, LANES), 0) // (2 * CHUNK)
    e_lane = lax.broadcasted_iota(jnp.int32, (4 * CHUNK, LANES), 1) // HEAD_DIM
    ones_cols = jnp.where(e_head == e_lane, 1.0, 0.0).astype(BF16)
    tiles_per_kv = (N_Q_HEADS // N_KV_HEADS) // HEADS_PER_VREG

    def sub_tile(r0):
        rows = slice(r0, r0 + SUB_MIX)
        n_blk = SUB_MIX // CHUNK
        b0 = r0 // CHUNK
        blocks = [slice(r0 + b * CHUNK, r0 + (b + 1) * CHUNK) for b in range(n_blk)]

        h = h_ref[rows, :]
        ha = h * (gn_ref[...] * (1.0 + mods_ref[mod_row + 1, seq_row, :]))
        ha_ref[rows, :] = ha.astype(BF16)
        r = lax.rsqrt(jnp.mean(h * h, axis=-1, keepdims=True) + EPS)
        qkv_ref[rows, :] = r * jnp.dot(ha_ref[rows, :], win_ref[:, qkv_cols],
                                       preferred_element_type=F32) + sproj_ref[seq_row, :]
        xn_ref[rows, :] = (ha * r + mods_ref[mod_row, seq_row, :]).astype(BF16)

        n_gate_slices = 2 * n_blk
        gate_w = 2 * D_MODEL // n_gate_slices
        gate_slices = iter(range(n_gate_slices))

        def gate_slice():
            c0 = next(gate_slices) * gate_w
            sg_ref[rows, c0:c0 + gate_w] = jax.nn.sigmoid(jnp.dot(
                xn_ref[rows, :], win_ref[:, COL_GATES + c0:COL_GATES + c0 + gate_w],
                preferred_element_type=F32))

        uv_ref[rows, :] = jnp.dot(xn_ref[rows, :], win_ref[:, COL_UV:COL_UV + 2 * D_A],
                                  preferred_element_type=F32)

        def mixer_a_block(rs):
            v = jax.nn.gelu(uv_ref[rs, D_A:])
            mu = jnp.mean(v, axis=-1, keepdims=True)
            var = jnp.mean(jnp.square(v - mu), axis=-1, keepdims=True)
            vn_ref[rs, :] = ((v - mu) * lax.rsqrt(var + EPS) * gln_ref[...] + bln_ref[...]).astype(BF16)
            uv_ref[rs, :D_A] = jax.nn.gelu(uv_ref[rs, :D_A])

        def spatial_gate():
            t_idx = lax.broadcasted_iota(jnp.int32, (CHUNK, CHUNK), 0)
            s_idx = lax.broadcasted_iota(jnp.int32, (CHUNK, CHUNK), 1)
            causal = s_idx <= t_idx
            for g in range(SGU_GROUPS):
                gs = slice(g * SGU_GROUP_DIM, (g + 1) * SGU_GROUP_DIM)
                w_g = jnp.where(causal, ws_ref[g], 0.0).astype(BF16)
                v_cat = jnp.concatenate([vn_ref[rs, gs] for rs in blocks], axis=1)
                z = jnp.dot(w_g, v_cat, preferred_element_type=F32)
                for b, rs in enumerate(blocks):
                    zb = z[:, b * SGU_GROUP_DIM:(b + 1) * SGU_GROUP_DIM] + bs_ref[:, gs]
                    ya_ref[rs, gs] = (uv_ref[rs, gs] * zb).astype(BF16)

        for b, rs in enumerate(blocks):
            qn_ref[rs, :] = (_head_rms(qkv_ref[rs, :D_B], gq_ref[...])
                             * (HEAD_DIM ** -0.5)).astype(BF16)
            k_ref[(b0 + b + 1) * CHUNK:(b0 + b + 2) * CHUNK, :] = _head_rms(
                qkv_ref[rs, D_B:D_B + D_KV], gk_ref[...])
            v_ref[(b0 + b + 1) * CHUNK:(b0 + b + 2) * CHUNK, :] = qkv_ref[rs, D_B + D_KV:]

        def attention_block(b, rs):
            kblk = _pair_blocks(k_ref[(b0 + b) * CHUNK:(b0 + b + 2) * CHUNK, :])
            vblk = _pair_blocks(v_ref[(b0 + b) * CHUNK:(b0 + b + 2) * CHUNK, :])
            for j in range(N_KV_HEADS):
                tiles = [j * tiles_per_kv + t for t in range(tiles_per_kv)]
                q_rows = jnp.concatenate([qn_ref[rs, t * LANES:(t + 1) * LANES] for t in tiles], axis=0)
                s = lax.dot_general(q_rows, kblk[j], (((1,), (1,)), ((), ())),
                                    preferred_element_type=F32)
                probs, sink_terms = [], []
                for i, t in enumerate(tiles):
                    p_blocks, e_sinks = [], []
                    for hh in range(HEADS_PER_VREG):
                        sink = sink_ref[t * HEADS_PER_VREG + hh]
                        col = hh * 2 * CHUNK
                        s_prev = s[i * CHUNK:(i + 1) * CHUNK, col:col + CHUNK]
                        s_own = s[i * CHUNK:(i + 1) * CHUNK, col + CHUNK:col + 2 * CHUNK]
                        if b0 + b == 0:
                            s_prev = jnp.minimum(s_prev, prev_cap)
                        folded = jnp.where(own_key, s_own, s_prev)
                        m = jnp.maximum(jnp.max(folded, axis=-1, keepdims=True), sink)
                        e = jnp.exp(folded - m)
                        p_blocks.append(jnp.where(own_key, 0.0, e).astype(BF16))
                        p_blocks.append(jnp.where(own_key, e, 0.0).astype(BF16))
                        e_sinks.append(jnp.exp(sink - m))
                    probs.append(jnp.concatenate(p_blocks, axis=1))
                    sink_terms.append(jnp.where(left, e_sinks[0], e_sinks[1]))
                pv = jnp.dot(jnp.concatenate(probs, axis=0),
                             jnp.concatenate([vblk[j], ones_cols], axis=1),
                             preferred_element_type=F32)
                out = pv[:, :LANES] / (pv[:, LANES:] + jnp.concatenate(sink_terms, axis=0))
                for i, t in enumerate(tiles):
                    attn_ref[rs, t * LANES:(t + 1) * LANES] = out[i * CHUNK:(i + 1) * CHUNK].astype(BF16)

        for b, rs in enumerate(blocks):
            mixer_a_block(rs)
            gate_slice()
            attention_block(b, rs)
            gate_slice()
        spatial_gate()

        for c in range(D_MODEL // MXU_DIM):
            cs = slice(c * MXU_DIM, (c + 1) * MXU_DIM)
            cs_b = slice(D_MODEL + c * MXU_DIM, D_MODEL + (c + 1) * MXU_DIM)
            y_a = jnp.dot(ya_ref[rows, :], wa_ref[:, cs], preferred_element_type=F32)
            y_b = jnp.dot(attn_ref[rows, :], wb_ref[:, cs], preferred_element_type=F32)
            mg_ref[rows, cs] = (sg_ref[rows, cs] * y_a + sg_ref[rows, cs_b] * y_b).astype(BF16)
        out = jnp.dot(mg_ref[rows, :], wo_ref[...], preferred_element_type=F32)
        o_ref[rows, :] = h_ref[rows, :] + mods_ref[mod_row + 2, seq_row, :] * out

    for r0 in range(0, tm, SUB_MIX):
        sub_tile(r0)

    k_ref[0:CHUNK, :] = k_ref[tm:tm + CHUNK, :]
    v_ref[0:CHUNK, :] = v_ref[tm:tm + CHUNK, :]


def _mixer(h2d, mods, g_norm, w_in, g_ln, b_ln, w_spatial, bias_rows, g_q, g_k, sinks,
           w_a, w_b, w_out, *, mod_row, seq, cast=()):
    n_tok = h2d.shape[0]
    n_mod_rows, bsz, _ = mods.shape
    n_steps = n_tok // TM_MIX
    tiles_per_seq = seq // TM_MIX
    row_spec = pl.BlockSpec((TM_MIX, D_MODEL), lambda i: (i, 0))
    cast_specs, cast_shapes = _cast_rider_specs(cast, n_steps)
    return pl.pallas_call(
        functools.partial(_mixer_kernel, mod_row=mod_row, tiles_per_seq=tiles_per_seq,
                          n_cast=len(cast)),
        out_shape=[jax.ShapeDtypeStruct((n_tok, D_MODEL), F32)] + cast_shapes,
        grid=(n_steps,),
        in_specs=[pl.BlockSpec(memory_space=pltpu.SMEM),
                  row_spec,
                  _const_spec((n_mod_rows, bsz, D_MODEL)),
                  _const_spec((1, D_MODEL)),
                  _const_spec((D_MODEL, IN_COLS)),
                  _const_spec((1, D_A)),
                  _const_spec((1, D_A)),
                  _const_spec((SGU_GROUPS, CHUNK, CHUNK)),
                  _const_spec((CHUNK, D_A)),
                  _const_spec((1, D_B)),
                  _const_spec((1, D_KV)),
                  _const_spec((D_A, D_MODEL)),
                  _const_spec((D_B, D_MODEL)),
                  _const_spec((D_MODEL, D_MODEL))] + cast_specs,
        out_specs=[row_spec] + cast_specs,
        scratch_shapes=[pltpu.VMEM((TM_MIX, D_MODEL), BF16),
                        pltpu.VMEM((TM_MIX, 2 * D_A), F32),
                        pltpu.VMEM((TM_MIX, D_A), BF16),
                        pltpu.VMEM((TM_MIX, D_A), BF16),
                        pltpu.VMEM((TM_MIX, D_B + 2 * D_KV), F32),
                        pltpu.VMEM((TM_MIX, D_B), BF16),
                        pltpu.VMEM((TM_MIX, D_B), BF16),
                        pltpu.VMEM((TM_MIX + CHUNK, D_KV), F32),
                        pltpu.VMEM((TM_MIX + CHUNK, D_KV), F32),
                        pltpu.VMEM((TM_MIX, 2 * D_MODEL), F32),
                        pltpu.VMEM((TM_MIX, D_MODEL), BF16),
                        pltpu.VMEM((TM_MIX, D_MODEL), BF16),
                        pltpu.VMEM((bsz, D_B + 2 * D_KV), F32)],
        compiler_params=pltpu.CompilerParams(
            dimension_semantics=("arbitrary",), vmem_limit_bytes=VMEM_LIMIT_MIX),
        name="mixer",
    )(sinks, h2d, mods, g_norm, w_in, g_ln, b_ln, w_spatial, bias_rows, g_q, g_k,
      w_a, w_b, w_out, *cast)


def kernel(x, c, w_ada, b_ada, g_norm1, ffn1_w_gate, ffn1_w_up, ffn1_w_down, g_norm2, w_in,
           g_sgu_ln, b_sgu_ln, w_spatial, b_spatial, g_q, g_k, attn_sinks, w_branch_a,
           w_branch_b, w_out, g_norm3, ffn2_w_gate, ffn2_w_up, ffn2_w_down):
    bsz, seq, d = x.shape
    assert d == D_MODEL and seq % TM_MIX == 0
    assert w_ada.shape[0] == 1, "single-layer problem"
    h = x.reshape(bsz * seq, d)
    b_ada2 = b_ada[0].reshape(1, N_MOD * d)

    mods = _adaln_mods(c, w_ada[0], b_ada2, N_MOD)

    h, w_in_b, w_a_b, w_b_b, w_out_b = _ffn(
        h, mods, g_norm1, ffn1_w_gate[0], ffn1_w_up[0], ffn1_w_down[0], mod_row=0, seq=seq,
        cast=(w_in[0], w_branch_a[0], w_branch_b[0], w_out[0]))

    bias_rows = jnp.repeat(b_spatial[0].T, SGU_GROUP_DIM, axis=1)
    h, w_gate_b, w_up_b, w_down_b = _mixer(
        h, mods, g_norm2, w_in_b, g_sgu_ln, b_sgu_ln, w_spatial[0], bias_rows,
        jnp.tile(g_q[0], N_Q_HEADS)[None], jnp.tile(g_k[0], N_KV_HEADS)[None], attn_sinks[0],
        w_a_b, w_b_b, w_out_b, mod_row=3, seq=seq,
        cast=(ffn2_w_gate[0], ffn2_w_up[0], ffn2_w_down[0]))

    (h,) = _ffn(h, mods, g_norm3, w_gate_b, w_up_b, w_down_b, mod_row=6, seq=seq)
    return h.reshape(bsz, seq, d)
```

```python
import functools

import jax
import jax.numpy as jnp
from jax import lax
from jax.experimental import pallas as pl
from jax.experimental.pallas import tpu as pltpu

F32 = jnp.float32
BF16 = jnp.bfloat16

D_MODEL = 1024
D_FF = 2816
CHUNK = 128
D_A = 512
SGU_GROUPS = 4
SGU_GROUP_DIM = D_A // SGU_GROUPS
HEAD_DIM = 64
N_Q_HEADS = 8
N_KV_HEADS = 2
D_B = N_Q_HEADS * HEAD_DIM
D_KV = N_KV_HEADS * HEAD_DIM
N_MOD = 9
EPS = 1e-6
NEG = -1e30

COL_UV = 0
COL_QKV = 2 * D_A
COL_GATES = COL_QKV + D_B + 2 * D_KV
IN_COLS = COL_GATES + 2 * D_MODEL

LANES = 128
MXU_DIM = 256
HEADS_PER_VREG = LANES // HEAD_DIM
BF16_SUBLANES = 16

TM_FFN = 1024
SUB_FFN_F32_WEIGHTS = 512
SUB_FFN_BF16_WEIGHTS = 1024
TM_MIX = 1024
SUB_MIX = 512
FF_CHUNK = MXU_DIM
VMEM_LIMIT_FFN_F32_WEIGHTS = 61 * 1024 * 1024
VMEM_LIMIT_FFN_BF16_WEIGHTS = 52 * 1024 * 1024
VMEM_LIMIT_MIX = 52 * 1024 * 1024
MODS_ROWS_PER_STEP = 3
VMEM_LIMIT_MODS = 32 * 1024 * 1024


def _const_spec(shape):
    zeros = (0,) * len(shape)
    return pl.BlockSpec(shape, lambda i: zeros, pipeline_mode=pl.Buffered(1))


def _dot(x, w):
    return lax.dot_general(x, w, (((1,), (0,)), ((), ())), preferred_element_type=F32)


def _cast_rider_specs(weights, n_steps):
    specs, shapes = [], []
    for w in weights:
        n_rows, n_cols = w.shape
        rows = BF16_SUBLANES
        while n_rows % rows or n_rows // rows > n_steps:
            rows += BF16_SUBLANES
        last = n_rows // rows - 1
        specs.append(pl.BlockSpec((rows, n_cols), lambda i, last=last: (jnp.minimum(i, last), 0)))
        shapes.append(jax.ShapeDtypeStruct(w.shape, BF16))
    return specs, shapes


def _run_cast_riders(in_refs, out_refs):
    for src, dst in zip(in_refs, out_refs):
        dst[...] = src[...].astype(BF16)


def _mods_kernel(c_ref, w_ref, b_ref, o_ref):
    rows = _dot(jax.nn.silu(c_ref[...]), w_ref[...]) + b_ref[...]
    for k in range(o_ref.shape[0]):
        o_ref[k] = rows[:, k * D_MODEL:(k + 1) * D_MODEL]


def _adaln_mods(c, w_ada, b_ada2, n_rows):
    bsz = c.shape[0]
    per_step = MODS_ROWS_PER_STEP
    assert n_rows % per_step == 0
    return pl.pallas_call(
        _mods_kernel,
        out_shape=jax.ShapeDtypeStruct((n_rows, bsz, D_MODEL), F32),
        grid=(n_rows // per_step,),
        in_specs=[pl.BlockSpec((bsz, D_MODEL), lambda j: (0, 0)),
                  pl.BlockSpec((D_MODEL, per_step * D_MODEL), lambda j: (0, j)),
                  pl.BlockSpec((1, per_step * D_MODEL), lambda j: (0, j))],
        out_specs=pl.BlockSpec((per_step, bsz, D_MODEL), lambda j: (j, 0, 0)),
        compiler_params=pltpu.CompilerParams(dimension_semantics=("arbitrary",),
                                             vmem_limit_bytes=VMEM_LIMIT_MODS),
        name="adaln_mods",
    )(c, w_ada, b_ada2)


def _ffn_kernel(*refs, mod_row, tiles_per_seq, n_cast):
    refs = list(refs)
    h_ref, mods_ref, gn_ref, wg_ref, wu_ref, wd_ref = refs[:6]
    del refs[:6]
    cast_in = refs[:n_cast]
    del refs[:n_cast]
    o_ref = refs.pop(0)
    cast_out = refs[:n_cast]
    del refs[:n_cast]
    xn_ref, ha_ref, act_ref, sproj_ref = refs

    step = pl.program_id(0)
    c0 = slice(0, FF_CHUNK)
    sub = xn_ref.shape[0]

    @pl.when(step == 0)
    def _():
        sproj_ref[:, :FF_CHUNK] = _dot(mods_ref[mod_row], wg_ref[:, c0])
        sproj_ref[:, FF_CHUNK:] = _dot(mods_ref[mod_row], wu_ref[:, c0])

    _run_cast_riders(cast_in, cast_out)
    seq_row = pl.ds(step // tiles_per_seq, 1)
    shift = mods_ref[mod_row, seq_row, :]
    gate = mods_ref[mod_row + 2, seq_row, :]

    for r0 in range(0, h_ref.shape[0], sub):
        rows = slice(r0, r0 + sub)
        h = h_ref[rows, :]
        ha_ref[...] = h * (gn_ref[...] * (1.0 + mods_ref[mod_row + 1, seq_row, :]))
        r = lax.rsqrt(jnp.mean(h * h, axis=-1, keepdims=True) + EPS)
        ha = ha_ref[...]
        g = r * _dot(ha, wg_ref[:, c0]) + sproj_ref[seq_row, :FF_CHUNK]
        u = r * _dot(ha, wu_ref[:, c0]) + sproj_ref[seq_row, FF_CHUNK:]
        act_ref[:, c0] = jax.nn.silu(g) * u
        xn_ref[...] = ha_ref[...] * r + shift
        for c in range(1, D_FF // FF_CHUNK):
            cs = slice(c * FF_CHUNK, (c + 1) * FF_CHUNK)
            xn = xn_ref[...]
            g = _dot(xn, wg_ref[:, cs])
            u = _dot(xn, wu_ref[:, cs])
            act_ref[:, cs] = jax.nn.silu(g) * u
        ff = _dot(act_ref[...], wd_ref[...])
        o_ref[rows, :] = h_ref[rows, :] + 0.5 * gate * ff


def _ffn(h2d, mods, g_norm, w_gate, w_up, w_down, *, mod_row, seq, cast=()):
    n_tok = h2d.shape[0]
    n_mod_rows, bsz, _ = mods.shape
    tm = TM_FFN
    bf16_weights = w_gate.dtype == BF16
    sub = SUB_FFN_BF16_WEIGHTS if bf16_weights else SUB_FFN_F32_WEIGHTS
    vmem_limit = VMEM_LIMIT_FFN_BF16_WEIGHTS if bf16_weights else VMEM_LIMIT_FFN_F32_WEIGHTS
    assert seq % tm == 0 and tm % sub == 0
    n_steps = n_tok // tm
    tiles_per_seq = seq // tm
    row_spec = pl.BlockSpec((tm, D_MODEL), lambda i: (i, 0))
    cast_specs, cast_shapes = _cast_rider_specs(cast, n_steps)
    in_specs = [row_spec,
                _const_spec((n_mod_rows, bsz, D_MODEL)),
                _const_spec((1, D_MODEL)),
                _const_spec(w_gate.shape),
                _const_spec(w_up.shape),
                _const_spec(w_down.shape)] + cast_specs
    out_specs = [row_spec] + cast_specs
    out_shapes = [jax.ShapeDtypeStruct((n_tok, D_MODEL), F32)] + cast_shapes
    operands = [h2d, mods, g_norm, w_gate, w_up, w_down, *cast]
    return pl.pallas_call(
        functools.partial(_ffn_kernel, mod_row=mod_row, tiles_per_seq=tiles_per_seq,
                          n_cast=len(cast)),
        out_shape=out_shapes,
        grid=(n_steps,),
        in_specs=in_specs,
        out_specs=out_specs,
        scratch_shapes=[pltpu.VMEM((sub, D_MODEL), F32),
                        pltpu.VMEM((sub, D_MODEL), F32),
                        pltpu.VMEM((sub, D_FF), F32),
                        pltpu.VMEM((bsz, 2 * FF_CHUNK), F32)],
        compiler_params=pltpu.CompilerParams(
            dimension_semantics=("arbitrary",), vmem_limit_bytes=vmem_limit),
        name=f"ffn_mod{mod_row}",
    )(*operands)


def _head_rms(x, gain):
    lane = lax.broadcasted_iota(jnp.int32, (x.shape[0], LANES), 1)
    left = lane < HEAD_DIM
    cols = []
    for t in range(x.shape[1] // LANES):
        xt = x[:, t * LANES:(t + 1) * LANES]
        sq = xt * xt
        s_all = jnp.sum(sq, axis=-1, keepdims=True)
        s_left = jnp.sum(jnp.where(left, sq, 0.0), axis=-1, keepdims=True)
        ms = jnp.where(left, s_left, s_all - s_left) * (1.0 / HEAD_DIM)
        cols.append(xt * lax.rsqrt(ms + EPS))
    return jnp.concatenate(cols, axis=1) * gain


def _pair_blocks(x):
    lane = lax.broadcasted_iota(jnp.int32, x.shape, 1)
    left = lane < HEAD_DIM
    xr = pltpu.roll(x, HEAD_DIM, axis=1)
    blk0 = jnp.concatenate([jnp.where(left, x, 0.0), jnp.where(left, 0.0, xr)], axis=0)
    blk1 = jnp.concatenate([jnp.where(left, xr, 0.0), jnp.where(left, 0.0, x)], axis=0)
    return blk0.astype(BF16), blk1.astype(BF16)


def _mixer_kernel(*refs, mod_row, tiles_per_seq, n_cast):
    refs = list(refs)
    (sink_ref, h_ref, mods_ref, gn_ref, win_ref, gln_ref, bln_ref, ws_ref, bs_ref, gq_ref, gk_ref,
     wa_ref, wb_ref, wo_ref) = refs[:14]
    del refs[:14]
    cast_in = refs[:n_cast]
    del refs[:n_cast]
    o_ref = refs.pop(0)
    cast_out = refs[:n_cast]
    del refs[:n_cast]
    (xn_ref, uv_ref, vn_ref, ya_ref, qkv_ref, qn_ref, attn_ref, k_ref, v_ref,
     sg_ref, mg_ref, ha_ref, sproj_ref) = refs
    tm = h_ref.shape[0]
    first_tile = (pl.program_id(0) % tiles_per_seq) == 0

    @pl.when(first_tile)
    def _():
        k_ref[0:CHUNK, :] = jnp.zeros((CHUNK, D_KV), F32)
        v_ref[0:CHUNK, :] = jnp.zeros((CHUNK, D_KV), F32)

    qkv_cols = slice(COL_QKV, COL_QKV + D_B + 2 * D_KV)

    @pl.when(pl.program_id(0) == 0)
    def _():
        sproj_ref[...] = _dot(mods_ref[mod_row].astype(BF16), win_ref[:, qkv_cols])

    _run_cast_riders(cast_in, cast_out)
    seq_row = pl.ds(pl.program_id(0) // tiles_per_seq, 1)

    own_key = (lax.broadcasted_iota(jnp.int32, (CHUNK, CHUNK), 1)
               <= lax.broadcasted_iota(jnp.int32, (CHUNK, CHUNK), 0))
    prev_cap = jnp.where(first_tile, NEG, jnp.finfo(F32).max)
    lane = lax.broadcasted_iota(jnp.int32, (CHUNK, LANES), 1)
    left = lane < HEAD_DIM
    e_head = lax.broadcasted_iota(jnp.int32, (4 * CHUNK, LANES), 0) // (2 * CHUNK)
    e_lane = lax.broadcasted_iota(jnp.int32, (4 * CHUNK, LANES), 1) // HEAD_DIM
    ones_cols = jnp.where(e_head == e_lane, 1.0, 0.0).astype(BF16)
    tiles_per_kv = (N_Q_HEADS // N_KV_HEADS) // HEADS_PER_VREG

    def sub_tile(r0):
        rows = slice(r0, r0 + SUB_MIX)
        n_blk = SUB_MIX // CHUNK
        b0 = r0 // CHUNK
        blocks = [slice(r0 + b * CHUNK, r0 + (b + 1) * CHUNK) for b in range(n_blk)]

        h = h_ref[rows, :]
        ha = h * (gn_ref[...] * (1.0 + mods_ref[mod_row + 1, seq_row, :]))
        ha_ref[rows, :] = ha.astype(BF16)
        r = lax.rsqrt(jnp.mean(h * h, axis=-1, keepdims=True) + EPS)
        qkv_ref[rows, :] = r * jnp.dot(ha_ref[rows, :], win_ref[:, qkv_cols],
                                       preferred_element_type=F32) + sproj_ref[seq_row, :]
        xn_ref[rows, :] = (ha * r + mods_ref[mod_row, seq_row, :]).astype(BF16)

        n_gate_slices = 2 * n_blk
        gate_w = 2 * D_MODEL // n_gate_slices
        gate_slices = iter(range(n_gate_slices))

        def gate_slice():
            c0 = next(gate_slices) * gate_w
            sg_ref[rows, c0:c0 + gate_w] = jax.nn.sigmoid(jnp.dot(
                xn_ref[rows, :], win_ref[:, COL_GATES + c0:COL_GATES + c0 + gate_w],
                preferred_element_type=F32))

        uv_ref[rows, :] = jnp.dot(xn_ref[rows, :], win_ref[:, COL_UV:COL_UV + 2 * D_A],
                                  preferred_element_type=F32)

        def mixer_a_block(rs):
            v = jax.nn.gelu(uv_ref[rs, D_A:])
            mu = jnp.mean(v, axis=-1, keepdims=True)
            var = jnp.mean(jnp.square(v - mu), axis=-1, keepdims=True)
            vn_ref[rs, :] = ((v - mu) * lax.rsqrt(var + EPS) * gln_ref[...] + bln_ref[...]).astype(BF16)
            uv_ref[rs, :D_A] = jax.nn.gelu(uv_ref[rs, :D_A])

        def spatial_gate():
            t_idx = lax.broadcasted_iota(jnp.int32, (CHUNK, CHUNK), 0)
            s_idx = lax.broadcasted_iota(jnp.int32, (CHUNK, CHUNK), 1)
            causal = s_idx <= t_idx
            for g in range(SGU_GROUPS):
                gs = slice(g * SGU_GROUP_DIM, (g + 1) * SGU_GROUP_DIM)
                w_g = jnp.where(causal, ws_ref[g], 0.0).astype(BF16)
                v_cat = jnp.concatenate([vn_ref[rs, gs] for rs in blocks], axis=1)
                z = jnp.dot(w_g, v_cat, preferred_element_type=F32)
                for b, rs in enumerate(blocks):
                    zb = z[:, b * SGU_GROUP_DIM:(b + 1) * SGU_GROUP_DIM] + bs_ref[:, gs]
                    ya_ref[rs, gs] = (uv_ref[rs, gs] * zb).astype(BF16)

        for b, rs in enumerate(blocks):
            qn_ref[rs, :] = (_head_rms(qkv_ref[rs, :D_B], gq_ref[...])
                             * (HEAD_DIM ** -0.5)).astype(BF16)
            k_ref[(b0 + b + 1) * CHUNK:(b0 + b + 2) * CHUNK, :] = _head_rms(
                qkv_ref[rs, D_B:D_B + D_KV], gk_ref[...])
            v_ref[(b0 + b + 1) * CHUNK:(b0 + b + 2) * CHUNK, :] = qkv_ref[rs, D_B + D_KV:]

        def attention_block(b, rs):
            kblk = _pair_blocks(k_ref[(b0 + b) * CHUNK:(b0 + b + 2) * CHUNK, :])
            vblk = _pair_blocks(v_ref[(b0 + b) * CHUNK:(b0 + b + 2) * CHUNK, :])
            for j in range(N_KV_HEADS):
                tiles = [j * tiles_per_kv + t for t in range(tiles_per_kv)]
                q_rows = jnp.concatenate([qn_ref[rs, t * LANES:(t + 1) * LANES] for t in tiles], axis=0)
                s = lax.dot_general(q_rows, kblk[j], (((1,), (1,)), ((), ())),
                                    preferred_element_type=F32)
                probs, sink_terms = [], []
                for i, t in enumerate(tiles):
                    p_blocks, e_sinks = [], []
                    for hh in range(HEADS_PER_VREG):
                        sink = sink_ref[t * HEADS_PER_VREG + hh]
                        col = hh * 2 * CHUNK
                        s_prev = s[i * CHUNK:(i + 1) * CHUNK, col:col + CHUNK]
                        s_own = s[i * CHUNK:(i + 1) * CHUNK, col + CHUNK:col + 2 * CHUNK]
                        if b0 + b == 0:
                            s_prev = jnp.minimum(s_prev, prev_cap)
                        folded = jnp.where(own_key, s_own, s_prev)
                        m = jnp.maximum(jnp.max(folded, axis=-1, keepdims=True), sink)
                        e = jnp.exp(folded - m)
                        p_blocks.append(jnp.where(own_key, 0.0, e).astype(BF16))
                        p_blocks.append(jnp.where(own_key, e, 0.0).astype(BF16))
                        e_sinks.append(jnp.exp(sink - m))
                    probs.append(jnp.concatenate(p_blocks, axis=1))
                    sink_terms.append(jnp.where(left, e_sinks[0], e_sinks[1]))
                pv = jnp.dot(jnp.concatenate(probs, axis=0),
                             jnp.concatenate([vblk[j], ones_cols], axis=1),
                             preferred_element_type=F32)
                out = pv[:, :LANES] / (pv[:, LANES:] + jnp.concatenate(sink_terms, axis=0))
                for i, t in enumerate(tiles):
                    attn_ref[rs, t * LANES:(t + 1) * LANES] = out[i * CHUNK:(i + 1) * CHUNK].astype(BF16)

        for b, rs in enumerate(blocks):
            mixer_a_block(rs)
            gate_slice()
            attention_block(b, rs)
            gate_slice()
        spatial_gate()

        for c in range(D_MODEL // MXU_DIM):
            cs = slice(c * MXU_DIM, (c + 1) * MXU_DIM)
            cs_b = slice(D_MODEL + c * MXU_DIM, D_MODEL + (c + 1) * MXU_DIM)
            y_a = jnp.dot(ya_ref[rows, :], wa_ref[:, cs], preferred_element_type=F32)
            y_b = jnp.dot(attn_ref[rows, :], wb_ref[:, cs], preferred_element_type=F32)
            mg_ref[rows, cs] = (sg_ref[rows, cs] * y_a + sg_ref[rows, cs_b] * y_b).astype(BF16)
        out = jnp.dot(mg_ref[rows, :], wo_ref[...], preferred_element_type=F32)
        o_ref[rows, :] = h_ref[rows, :] + mods_ref[mod_row + 2, seq_row, :] * out

    for r0 in range(0, tm, SUB_MIX):
        sub_tile(r0)

    k_ref[0:CHUNK, :] = k_ref[tm:tm + CHUNK, :]
    v_ref[0:CHUNK, :] = v_ref[tm:tm + CHUNK, :]


def _mixer(h2d, mods, g_norm, w_in, g_ln, b_ln, w_spatial, bias_rows, g_q, g_k, sinks,
           w_a, w_b, w_out, *, mod_row, seq, cast=()):
    n_tok = h2d.shape[0]
    n_mod_rows, bsz, _ = mods.shape
    n_steps = n_tok // TM_MIX
    tiles_per_seq = seq // TM_MIX
    row_spec = pl.BlockSpec((TM_MIX, D_MODEL), lambda i: (i, 0))
    cast_specs, cast_shapes = _cast_rider_specs(cast, n_steps)
    return pl.pallas_call(
        functools.partial(_mixer_kernel, mod_row=mod_row, tiles_per_seq=tiles_per_seq,
                          n_cast=len(cast)),
        out_shape=[jax.ShapeDtypeStruct((n_tok, D_MODEL), F32)] + cast_shapes,
        grid=(n_steps,),
        in_specs=[pl.BlockSpec(memory_space=pltpu.SMEM),
                  row_spec,
                  _const_spec((n_mod_rows, bsz, D_MODEL)),
                  _const_spec((1, D_MODEL)),
                  _const_spec((D_MODEL, IN_COLS)),
                  _const_spec((1, D_A)),
                  _const_spec((1, D_A)),
                  _const_spec((SGU_GROUPS, CHUNK, CHUNK)),
                  _const_spec((CHUNK, D_A)),
                  _const_spec((1, D_B)),
                  _const_spec((1, D_KV)),
                  _const_spec((D_A, D_MODEL)),
                  _const_spec((D_B, D_MODEL)),
                  _const_spec((D_MODEL, D_MODEL))] + cast_specs,
        out_specs=[row_spec] + cast_specs,
        scratch_shapes=[pltpu.VMEM((TM_MIX, D_MODEL), BF16),
                        pltpu.VMEM((TM_MIX, 2 * D_A), F32),
                        pltpu.VMEM((TM_MIX, D_A), BF16),
                        pltpu.VMEM((TM_MIX, D_A), BF16),
                        pltpu.VMEM((TM_MIX, D_B + 2 * D_KV), F32),
                        pltpu.VMEM((TM_MIX, D_B), BF16),
                        pltpu.VMEM((TM_MIX, D_B), BF16),
                        pltpu.VMEM((TM_MIX + CHUNK, D_KV), F32),
                        pltpu.VMEM((TM_MIX + CHUNK, D_KV), F32),
                        pltpu.VMEM((TM_MIX, 2 * D_MODEL), F32),
                        pltpu.VMEM((TM_MIX, D_MODEL), BF16),
                        pltpu.VMEM((TM_MIX, D_MODEL), BF16),
                        pltpu.VMEM((bsz, D_B + 2 * D_KV), F32)],
        compiler_params=pltpu.CompilerParams(
            dimension_semantics=("arbitrary",), vmem_limit_bytes=VMEM_LIMIT_MIX),
        name="mixer",
    )(sinks, h2d, mods, g_norm, w_in, g_ln, b_ln, w_spatial, bias_rows, g_q, g_k,
      w_a, w_b, w_out, *cast)


def kernel(x, c, w_ada, b_ada, g_norm1, ffn1_w_gate, ffn1_w_up, ffn1_w_down, g_norm2, w_in,
           g_sgu_ln, b_sgu_ln, w_spatial, b_spatial, g_q, g_k, attn_sinks, w_branch_a,
           w_branch_b, w_out, g_norm3, ffn2_w_gate, ffn2_w_up, ffn2_w_down):
    bsz, seq, d = x.shape
    assert d == D_MODEL and seq % TM_MIX == 0
    assert w_ada.shape[0] == 1, "single-layer problem"
    h = x.reshape(bsz * seq, d)
    b_ada2 = b_ada[0].reshape(1, N_MOD * d)

    mods = _adaln_mods(c, w_ada[0], b_ada2, N_MOD)

    h, w_in_b, w_a_b, w_b_b, w_out_b = _ffn(
        h, mods, g_norm1, ffn1_w_gate[0], ffn1_w_up[0], ffn1_w_down[0], mod_row=0, seq=seq,
        cast=(w_in[0], w_branch_a[0], w_branch_b[0], w_out[0]))

    bias_rows = jnp.repeat(b_spatial[0].T, SGU_GROUP_DIM, axis=1)
    h, w_gate_b, w_up_b, w_down_b = _mixer(
        h, mods, g_norm2, w_in_b, g_sgu_ln, b_sgu_ln, w_spatial[0], bias_rows,
        jnp.tile(g_q[0], N_Q_HEADS)[None], jnp.tile(g_k[0], N_KV_HEADS)[None], attn_sinks[0],
        w_a_b, w_b_b, w_out_b, mod_row=3, seq=seq,
        cast=(ffn2_w_gate[0], ffn2_w_up[0], ffn2_w_down[0]))

    (h,) = _ffn(h, mods, g_norm3, w_gate_b, w_up_b, w_down_b, mod_row=6, seq=seq)
    return h.reshape(bsz, seq, d)
```

```python
import functools

import jax
import jax.numpy as jnp
from jax import lax
from jax.experimental import pallas as pl
from jax.experimental.pallas import tpu as pltpu

F32 = jnp.float32
BF16 = jnp.bfloat16

D_MODEL = 1024
D_FF = 2816
CHUNK = 128
D_A = 512
SGU_GROUPS = 4
SGU_GROUP_DIM = D_A // SGU_GROUPS
HEAD_DIM = 64
N_Q_HEADS = 8
N_KV_HEADS = 2
D_B = N_Q_HEADS * HEAD_DIM
D_KV = N_KV_HEADS * HEAD_DIM
N_MOD = 9
EPS = 1e-6
NEG = -1e30

COL_UV = 0
COL_QKV = 2 * D_A
COL_GATES = COL_QKV + D_B + 2 * D_KV
IN_COLS = COL_GATES + 2 * D_MODEL

LANES = 128
MXU_DIM = 256
HEADS_PER_VREG = LANES // HEAD_DIM
BF16_SUBLANES = 16

TM_FFN = 1024
SUB_FFN_F32_WEIGHTS = 512
SUB_FFN_BF16_WEIGHTS = 1024
TM_MIX = 1024
SUB_MIX = 512
FF_CHUNK = MXU_DIM
VMEM_LIMIT_FFN_F32_WEIGHTS = 61 * 1024 * 1024
VMEM_LIMIT_FFN_BF16_WEIGHTS = 52 * 1024 * 1024
VMEM_LIMIT_MIX = 52 * 1024 * 1024


def _const_spec(shape):
    zeros = (0,) * len(shape)
    return pl.BlockSpec(shape, lambda i: zeros, pipeline_mode=pl.Buffered(1))


def _dot(x, w):
    return lax.dot_general(x, w, (((1,), (0,)), ((), ())), preferred_element_type=F32)


def _cast_rider_specs(weights, n_steps):
    specs, shapes = [], []
    for w in weights:
        n_rows, n_cols = w.shape
        rows = BF16_SUBLANES
        while n_rows % rows or n_rows // rows > n_steps:
            rows += BF16_SUBLANES
        last = n_rows // rows - 1
        specs.append(pl.BlockSpec((rows, n_cols), lambda i, last=last: (jnp.minimum(i, last), 0)))
        shapes.append(jax.ShapeDtypeStruct(w.shape, BF16))
    return specs, shapes


def _run_cast_riders(in_refs, out_refs):
    for src, dst in zip(in_refs, out_refs):
        dst[...] = src[...].astype(BF16)


def _mods_kernel(c_ref, w_ref, b_ref, o_ref):
    o_ref[0] = _dot(jax.nn.silu(c_ref[...]), w_ref[...]) + b_ref[...]


def _adaln_mods(c, w_ada, b_ada2, n_rows):
    bsz = c.shape[0]
    return pl.pallas_call(
        _mods_kernel,
        out_shape=jax.ShapeDtypeStruct((n_rows, bsz, D_MODEL), F32),
        grid=(n_rows,),
        in_specs=[pl.BlockSpec((bsz, D_MODEL), lambda j: (0, 0)),
                  pl.BlockSpec((D_MODEL, D_MODEL), lambda j: (0, j)),
                  pl.BlockSpec((1, D_MODEL), lambda j: (0, j))],
        out_specs=pl.BlockSpec((1, bsz, D_MODEL), lambda j: (j, 0, 0)),
        compiler_params=pltpu.CompilerParams(dimension_semantics=("arbitrary",)),
        name="adaln_mods",
    )(c, w_ada, b_ada2)


def _ffn_kernel(*refs, mod_row, tiles_per_seq, n_cast):
    refs = list(refs)
    h_ref, mods_ref, gn_ref, wg_ref, wu_ref, wd_ref = refs[:6]
    del refs[:6]
    cast_in = refs[:n_cast]
    del refs[:n_cast]
    o_ref = refs.pop(0)
    cast_out = refs[:n_cast]
    del refs[:n_cast]
    xn_ref, ha_ref, act_ref, sproj_ref = refs

    step = pl.program_id(0)
    c0 = slice(0, FF_CHUNK)
    sub = xn_ref.shape[0]

    @pl.when(step == 0)
    def _():
        sproj_ref[:, :FF_CHUNK] = _dot(mods_ref[mod_row], wg_ref[:, c0])
        sproj_ref[:, FF_CHUNK:] = _dot(mods_ref[mod_row], wu_ref[:, c0])

    _run_cast_riders(cast_in, cast_out)
    seq_row = pl.ds(step // tiles_per_seq, 1)
    shift = mods_ref[mod_row, seq_row, :]
    gate = mods_ref[mod_row + 2, seq_row, :]

    for r0 in range(0, h_ref.shape[0], sub):
        rows = slice(r0, r0 + sub)
        h = h_ref[rows, :]
        ha_ref[...] = h * (gn_ref[...] * (1.0 + mods_ref[mod_row + 1, seq_row, :]))
        r = lax.rsqrt(jnp.mean(h * h, axis=-1, keepdims=True) + EPS)
        ha = ha_ref[...]
        g = r * _dot(ha, wg_ref[:, c0]) + sproj_ref[seq_row, :FF_CHUNK]
        u = r * _dot(ha, wu_ref[:, c0]) + sproj_ref[seq_row, FF_CHUNK:]
        act_ref[:, c0] = jax.nn.silu(g) * u
        xn_ref[...] = ha_ref[...] * r + shift
        for c in range(1, D_FF // FF_CHUNK):
            cs = slice(c * FF_CHUNK, (c + 1) * FF_CHUNK)
            xn = xn_ref[...]
            g = _dot(xn, wg_ref[:, cs])
            u = _dot(xn, wu_ref[:, cs])
            act_ref[:, cs] = jax.nn.silu(g) * u
        ff = _dot(act_ref[...], wd_ref[...])
        o_ref[rows, :] = h_ref[rows, :] + 0.5 * gate * ff


def _ffn(h2d, mods, g_norm, w_gate, w_up, w_down, *, mod_row, seq, cast=()):
    n_tok = h2d.shape[0]
    n_mod_rows, bsz, _ = mods.shape
    tm = TM_FFN
    bf16_weights = w_gate.dtype == BF16
    sub = SUB_FFN_BF16_WEIGHTS if bf16_weights else SUB_FFN_F32_WEIGHTS
    vmem_limit = VMEM_LIMIT_FFN_BF16_WEIGHTS if bf16_weights else VMEM_LIMIT_FFN_F32_WEIGHTS
    assert seq % tm == 0 and tm % sub == 0
    n_steps = n_tok // tm
    tiles_per_seq = seq // tm
    row_spec = pl.BlockSpec((tm, D_MODEL), lambda i: (i, 0))
    cast_specs, cast_shapes = _cast_rider_specs(cast, n_steps)
    in_specs = [row_spec,
                _const_spec((n_mod_rows, bsz, D_MODEL)),
                _const_spec((1, D_MODEL)),
                _const_spec(w_gate.shape),
                _const_spec(w_up.shape),
                _const_spec(w_down.shape)] + cast_specs
    out_specs = [row_spec] + cast_specs
    out_shapes = [jax.ShapeDtypeStruct((n_tok, D_MODEL), F32)] + cast_shapes
    operands = [h2d, mods, g_norm, w_gate, w_up, w_down, *cast]
    return pl.pallas_call(
        functools.partial(_ffn_kernel, mod_row=mod_row, tiles_per_seq=tiles_per_seq,
                          n_cast=len(cast)),
        out_shape=out_shapes,
        grid=(n_steps,),
        in_specs=in_specs,
        out_specs=out_specs,
        scratch_shapes=[pltpu.VMEM((sub, D_MODEL), F32),
                        pltpu.VMEM((sub, D_MODEL), F32),
                        pltpu.VMEM((sub, D_FF), F32),
                        pltpu.VMEM((bsz, 2 * FF_CHUNK), F32)],
        compiler_params=pltpu.CompilerParams(
            dimension_semantics=("arbitrary",), vmem_limit_bytes=vmem_limit),
        name=f"ffn_mod{mod_row}",
    )(*operands)


def _head_rms(x, gain):
    lane = lax.broadcasted_iota(jnp.int32, (x.shape[0], LANES), 1)
    left = lane < HEAD_DIM
    cols = []
    for t in range(x.shape[1] // LANES):
        xt = x[:, t * LANES:(t + 1) * LANES]
        sq = xt * xt
        s_all = jnp.sum(sq, axis=-1, keepdims=True)
        s_left = jnp.sum(jnp.where(left, sq, 0.0), axis=-1, keepdims=True)
        ms = jnp.where(left, s_left, s_all - s_left) * (1.0 / HEAD_DIM)
        cols.append(xt * lax.rsqrt(ms + EPS))
    return jnp.concatenate(cols, axis=1) * gain


def _pair_blocks(x):
    lane = lax.broadcasted_iota(jnp.int32, x.shape, 1)
    left = lane < HEAD_DIM
    xr = pltpu.roll(x, HEAD_DIM, axis=1)
    blk0 = jnp.concatenate([jnp.where(left, x, 0.0), jnp.where(left, 0.0, xr)], axis=0)
    blk1 = jnp.concatenate([jnp.where(left, xr, 0.0), jnp.where(left, 0.0, x)], axis=0)
    return blk0.astype(BF16), blk1.astype(BF16)


def _mixer_kernel(*refs, mod_row, tiles_per_seq, n_cast):
    refs = list(refs)
    (sink_ref, h_ref, mods_ref, gn_ref, win_ref, gln_ref, bln_ref, ws_ref, bs_ref, gq_ref, gk_ref,
     wa_ref, wb_ref, wo_ref) = refs[:14]
    del refs[:14]
    cast_in = refs[:n_cast]
    del refs[:n_cast]
    o_ref = refs.pop(0)
    cast_out = refs[:n_cast]
    del refs[:n_cast]
    (xn_ref, uv_ref, vn_ref, ya_ref, qkv_ref, qn_ref, attn_ref, k_ref, v_ref,
     sg_ref, mg_ref, ha_ref, sproj_ref) = refs
    tm = h_ref.shape[0]
    first_tile = (pl.program_id(0) % tiles_per_seq) == 0

    @pl.when(first_tile)
    def _():
        k_ref[0:CHUNK, :] = jnp.zeros((CHUNK, D_KV), F32)
        v_ref[0:CHUNK, :] = jnp.zeros((CHUNK, D_KV), F32)

    qkv_cols = slice(COL_QKV, COL_QKV + D_B + 2 * D_KV)

    @pl.when(pl.program_id(0) == 0)
    def _():
        sproj_ref[...] = _dot(mods_ref[mod_row].astype(BF16), win_ref[:, qkv_cols])

    _run_cast_riders(cast_in, cast_out)
    seq_row = pl.ds(pl.program_id(0) // tiles_per_seq, 1)

    own_key = (lax.broadcasted_iota(jnp.int32, (CHUNK, CHUNK), 1)
               <= lax.broadcasted_iota(jnp.int32, (CHUNK, CHUNK), 0))
    prev_cap = jnp.where(first_tile, NEG, jnp.finfo(F32).max)
    lane = lax.broadcasted_iota(jnp.int32, (CHUNK, LANES), 1)
    left = lane < HEAD_DIM
    e_head = lax.broadcasted_iota(jnp.int32, (4 * CHUNK, LANES), 0) // (2 * CHUNK)
    e_lane = lax.broadcasted_iota(jnp.int32, (4 * CHUNK, LANES), 1) // HEAD_DIM
    ones_cols = jnp.where(e_head == e_lane, 1.0, 0.0).astype(BF16)
    tiles_per_kv = (N_Q_HEADS // N_KV_HEADS) // HEADS_PER_VREG

    def sub_tile(r0):
        rows = slice(r0, r0 + SUB_MIX)
        n_blk = SUB_MIX // CHUNK
        b0 = r0 // CHUNK
        blocks = [slice(r0 + b * CHUNK, r0 + (b + 1) * CHUNK) for b in range(n_blk)]

        h = h_ref[rows, :]
        ha = h * (gn_ref[...] * (1.0 + mods_ref[mod_row + 1, seq_row, :]))
        ha_ref[rows, :] = ha.astype(BF16)
        r = lax.rsqrt(jnp.mean(h * h, axis=-1, keepdims=True) + EPS)
        qkv_ref[rows, :] = r * jnp.dot(ha_ref[rows, :], win_ref[:, qkv_cols],
                                       preferred_element_type=F32) + sproj_ref[seq_row, :]
        xn_ref[rows, :] = (ha * r + mods_ref[mod_row, seq_row, :]).astype(BF16)

        n_gate_slices = 2 * n_blk
        gate_w = 2 * D_MODEL // n_gate_slices
        gate_slices = iter(range(n_gate_slices))

        def gate_slice():
            c0 = next(gate_slices) * gate_w
            sg_ref[rows, c0:c0 + gate_w] = jax.nn.sigmoid(jnp.dot(
                xn_ref[rows, :], win_ref[:, COL_GATES + c0:COL_GATES + c0 + gate_w],
                preferred_element_type=F32))

        uv_ref[rows, :] = jnp.dot(xn_ref[rows, :], win_ref[:, COL_UV:COL_UV + 2 * D_A],
                                  preferred_element_type=F32)

        def mixer_a_block(rs):
            v = jax.nn.gelu(uv_ref[rs, D_A:])
            mu = jnp.mean(v, axis=-1, keepdims=True)
            var = jnp.mean(jnp.square(v - mu), axis=-1, keepdims=True)
            vn_ref[rs, :] = ((v - mu) * lax.rsqrt(var + EPS) * gln_ref[...] + bln_ref[...]).astype(BF16)
            uv_ref[rs, :D_A] = jax.nn.gelu(uv_ref[rs, :D_A])

        def spatial_gate():
            t_idx = lax.broadcasted_iota(jnp.int32, (CHUNK, CHUNK), 0)
            s_idx = lax.broadcasted_iota(jnp.int32, (CHUNK, CHUNK), 1)
            causal = s_idx <= t_idx
            for g in range(SGU_GROUPS):
                gs = slice(g * SGU_GROUP_DIM, (g + 1) * SGU_GROUP_DIM)
                w_g = jnp.where(causal, ws_ref[g], 0.0).astype(BF16)
                v_cat = jnp.concatenate([vn_ref[rs, gs] for rs in blocks], axis=1)
                z = jnp.dot(w_g, v_cat, preferred_element_type=F32)
                for b, rs in enumerate(blocks):
                    zb = z[:, b * SGU_GROUP_DIM:(b + 1) * SGU_GROUP_DIM] + bs_ref[:, gs]
                    ya_ref[rs, gs] = (uv_ref[rs, gs] * zb).astype(BF16)

        for b, rs in enumerate(blocks):
            qn_ref[rs, :] = (_head_rms(qkv_ref[rs, :D_B], gq_ref[...])
                             * (HEAD_DIM ** -0.5)).astype(BF16)
            k_ref[(b0 + b + 1) * CHUNK:(b0 + b + 2) * CHUNK, :] = _head_rms(
                qkv_ref[rs, D_B:D_B + D_KV], gk_ref[...])
            v_ref[(b0 + b + 1) * CHUNK:(b0 + b + 2) * CHUNK, :] = qkv_ref[rs, D_B + D_KV:]

        def attention_block(b, rs):
            kblk = _pair_blocks(k_ref[(b0 + b) * CHUNK:(b0 + b + 2) * CHUNK, :])
            vblk = _pair_blocks(v_ref[(b0 + b) * CHUNK:(b0 + b + 2) * CHUNK, :])
            for j in range(N_KV_HEADS):
                tiles = [j * tiles_per_kv + t for t in range(tiles_per_kv)]
                q_rows = jnp.concatenate([qn_ref[rs, t * LANES:(t + 1) * LANES] for t in tiles], axis=0)
                s = lax.dot_general(q_rows, kblk[j], (((1,), (1,)), ((), ())),
                                    preferred_element_type=F32)
                probs, sink_terms = [], []
                for i, t in enumerate(tiles):
                    p_blocks, e_sinks = [], []
                    for hh in range(HEADS_PER_VREG):
                        sink = sink_ref[t * HEADS_PER_VREG + hh]
                        col = hh * 2 * CHUNK
                        s_prev = s[i * CHUNK:(i + 1) * CHUNK, col:col + CHUNK]
                        s_own = s[i * CHUNK:(i + 1) * CHUNK, col + CHUNK:col + 2 * CHUNK]
                        if b0 + b == 0:
                            s_prev = jnp.minimum(s_prev, prev_cap)
                        folded = jnp.where(own_key, s_own, s_prev)
                        m = jnp.maximum(jnp.max(folded, axis=-1, keepdims=True), sink)
                        e = jnp.exp(folded - m)
                        p_blocks.append(jnp.where(own_key, 0.0, e).astype(BF16))
                        p_blocks.append(jnp.where(own_key, e, 0.0).astype(BF16))
                        e_sinks.append(jnp.exp(sink - m))
                    probs.append(jnp.concatenate(p_blocks, axis=1))
                    sink_terms.append(jnp.where(left, e_sinks[0], e_sinks[1]))
                pv = jnp.dot(jnp.concatenate(probs, axis=0),
                             jnp.concatenate([vblk[j], ones_cols], axis=1),
                             preferred_element_type=F32)
                out = pv[:, :LANES] / (pv[:, LANES:] + jnp.concatenate(sink_terms, axis=0))
                for i, t in enumerate(tiles):
                    attn_ref[rs, t * LANES:(t + 1) * LANES] = out[i * CHUNK:(i + 1) * CHUNK].astype(BF16)

        for b, rs in enumerate(blocks):
            mixer_a_block(rs)
            gate_slice()
            attention_block(b, rs)
            gate_slice()
        spatial_gate()

        def merge_and_project():
            for c in range(D_MODEL // MXU_DIM):
                cs = slice(c * MXU_DIM, (c + 1) * MXU_DIM)
                cs_b = slice(D_MODEL + c * MXU_DIM, D_MODEL + (c + 1) * MXU_DIM)
                y_a = jnp.dot(ya_ref[rows, :], wa_ref[:, cs], preferred_element_type=F32)
                y_b = jnp.dot(attn_ref[rows, :], wb_ref[:, cs], preferred_element_type=F32)
                mg_ref[rows, cs] = (sg_ref[rows, cs] * y_a + sg_ref[rows, cs_b] * y_b).astype(BF16)
            out = jnp.dot(mg_ref[rows, :], wo_ref[...], preferred_element_type=F32)
            o_ref[rows, :] = h_ref[rows, :] + mods_ref[mod_row + 2, seq_row, :] * out

        return merge_and_project

    pending = None
    for r0 in range(0, tm, SUB_MIX):
        finish = sub_tile(r0)
        if pending is not None:
            pending()
        pending = finish
    pending()

    k_ref[0:CHUNK, :] = k_ref[tm:tm + CHUNK, :]
    v_ref[0:CHUNK, :] = v_ref[tm:tm + CHUNK, :]


def _mixer(h2d, mods, g_norm, w_in, g_ln, b_ln, w_spatial, bias_rows, g_q, g_k, sinks,
           w_a, w_b, w_out, *, mod_row, seq, cast=()):
    n_tok = h2d.shape[0]
    n_mod_rows, bsz, _ = mods.shape
    n_steps = n_tok // TM_MIX
    tiles_per_seq = seq // TM_MIX
    row_spec = pl.BlockSpec((TM_MIX, D_MODEL), lambda i: (i, 0))
    cast_specs, cast_shapes = _cast_rider_specs(cast, n_steps)
    return pl.pallas_call(
        functools.partial(_mixer_kernel, mod_row=mod_row, tiles_per_seq=tiles_per_seq,
                          n_cast=len(cast)),
        out_shape=[jax.ShapeDtypeStruct((n_tok, D_MODEL), F32)] + cast_shapes,
        grid=(n_steps,),
        in_specs=[pl.BlockSpec(memory_space=pltpu.SMEM),
                  row_spec,
                  _const_spec((n_mod_rows, bsz, D_MODEL)),
                  _const_spec((1, D_MODEL)),
                  _const_spec((D_MODEL, IN_COLS)),
                  _const_spec((1, D_A)),
                  _const_spec((1, D_A)),
                  _const_spec((SGU_GROUPS, CHUNK, CHUNK)),
                  _const_spec((CHUNK, D_A)),
                  _const_spec((1, D_B)),
                  _const_spec((1, D_KV)),
                  _const_spec((D_A, D_MODEL)),
                  _const_spec((D_B, D_MODEL)),
                  _const_spec((D_MODEL, D_MODEL))] + cast_specs,
        out_specs=[row_spec] + cast_specs,
        scratch_shapes=[pltpu.VMEM((TM_MIX, D_MODEL), BF16),
                        pltpu.VMEM((TM_MIX, 2 * D_A), F32),
                        pltpu.VMEM((TM_MIX, D_A), BF16),
                        pltpu.VMEM((TM_MIX, D_A), BF16),
                        pltpu.VMEM((TM_MIX, D_B + 2 * D_KV), F32),
                        pltpu.VMEM((TM_MIX, D_B), BF16),
                        pltpu.VMEM((TM_MIX, D_B), BF16),
                        pltpu.VMEM((TM_MIX + CHUNK, D_KV), F32),
                        pltpu.VMEM((TM_MIX + CHUNK, D_KV), F32),
                        pltpu.VMEM((TM_MIX, 2 * D_MODEL), F32),
                        pltpu.VMEM((TM_MIX, D_MODEL), BF16),
                        pltpu.VMEM((TM_MIX, D_MODEL), BF16),
                        pltpu.VMEM((bsz, D_B + 2 * D_KV), F32)],
        compiler_params=pltpu.CompilerParams(
            dimension_semantics=("arbitrary",), vmem_limit_bytes=VMEM_LIMIT_MIX),
        name="mixer",
    )(sinks, h2d, mods, g_norm, w_in, g_ln, b_ln, w_spatial, bias_rows, g_q, g_k,
      w_a, w_b, w_out, *cast)


def kernel(x, c, w_ada, b_ada, g_norm1, ffn1_w_gate, ffn1_w_up, ffn1_w_down, g_norm2, w_in,
           g_sgu_ln, b_sgu_ln, w_spatial, b_spatial, g_q, g_k, attn_sinks, w_branch_a,
           w_branch_b, w_out, g_norm3, ffn2_w_gate, ffn2_w_up, ffn2_w_down):
    bsz, seq, d = x.shape
    assert d == D_MODEL and seq % TM_MIX == 0
    assert w_ada.shape[0] == 1, "single-layer problem"
    h = x.reshape(bsz * seq, d)
    b_ada2 = b_ada[0].reshape(1, N_MOD * d)

    mods = _adaln_mods(c, w_ada[0], b_ada2, N_MOD)

    h, w_in_b, w_a_b, w_b_b, w_out_b = _ffn(
        h, mods, g_norm1, ffn1_w_gate[0], ffn1_w_up[0], ffn1_w_down[0], mod_row=0, seq=seq,
        cast=(w_in[0], w_branch_a[0], w_branch_b[0], w_out[0]))

    bias_rows = jnp.repeat(b_spatial[0].T, SGU_GROUP_DIM, axis=1)
    h, w_gate_b, w_up_b, w_down_b = _mixer(
        h, mods, g_norm2, w_in_b, g_sgu_ln, b_sgu_ln, w_spatial[0], bias_rows,
        jnp.tile(g_q[0], N_Q_HEADS)[None], jnp.tile(g_k[0], N_KV_HEADS)[None], attn_sinks[0],
        w_a_b, w_b_b, w_out_b, mod_row=3, seq=seq,
        cast=(ffn2_w_gate[0], ffn2_w_up[0], ffn2_w_down[0]))

    (h,) = _ffn(h, mods, g_norm3, w_gate_b, w_up_b, w_down_b, mod_row=6, seq=seq)
    return h.reshape(bsz, seq, d)
```

```python
import functools

import jax
import jax.numpy as jnp
from jax import lax
from jax.experimental import pallas as pl
from jax.experimental.pallas import tpu as pltpu

F32 = jnp.float32
BF16 = jnp.bfloat16

D_MODEL = 1024
D_FF = 2816
CHUNK = 128
D_A = 512
SGU_GROUPS = 4
SGU_GROUP_DIM = D_A // SGU_GROUPS
HEAD_DIM = 64
N_Q_HEADS = 8
N_KV_HEADS = 2
D_B = N_Q_HEADS * HEAD_DIM
D_KV = N_KV_HEADS * HEAD_DIM
N_MOD = 9
EPS = 1e-6
NEG = -1e30

COL_UV = 0
COL_QKV = 2 * D_A
COL_GATES = COL_QKV + D_B + 2 * D_KV
IN_COLS = COL_GATES + 2 * D_MODEL

LANES = 128
MXU_DIM = 256
HEADS_PER_VREG = LANES // HEAD_DIM
BF16_SUBLANES = 16

TM_FFN = 1024
SUB_FFN_F32_WEIGHTS = 512
SUB_FFN_BF16_WEIGHTS = 1024
TM_MIX = 1024
SUB_MIX = 512
FF_CHUNK = MXU_DIM
VMEM_LIMIT_FFN_F32_WEIGHTS = 61 * 1024 * 1024
VMEM_LIMIT_FFN_BF16_WEIGHTS = 52 * 1024 * 1024
VMEM_LIMIT_MIX = 52 * 1024 * 1024


def _const_spec(shape):
    zeros = (0,) * len(shape)
    return pl.BlockSpec(shape, lambda i: zeros, pipeline_mode=pl.Buffered(1))


def _dot(x, w):
    return lax.dot_general(x, w, (((1,), (0,)), ((), ())), preferred_element_type=F32)


def _cast_rider_specs(weights, n_steps):
    specs, shapes = [], []
    for w in weights:
        n_rows, n_cols = w.shape
        rows = BF16_SUBLANES
        while n_rows % rows or n_rows // rows > n_steps:
            rows += BF16_SUBLANES
        last = n_rows // rows - 1
        specs.append(pl.BlockSpec((rows, n_cols), lambda i, last=last: (jnp.minimum(i, last), 0)))
        shapes.append(jax.ShapeDtypeStruct(w.shape, BF16))
    return specs, shapes


def _run_cast_riders(in_refs, out_refs):
    for src, dst in zip(in_refs, out_refs):
        dst[...] = src[...].astype(BF16)


def _mods_kernel(c_ref, w_ref, b_ref, o_ref):
    o_ref[0] = _dot(jax.nn.silu(c_ref[...]), w_ref[...]) + b_ref[...]


def _adaln_mods(c, w_ada, b_ada2, n_rows):
    bsz = c.shape[0]
    return pl.pallas_call(
        _mods_kernel,
        out_shape=jax.ShapeDtypeStruct((n_rows, bsz, D_MODEL), F32),
        grid=(n_rows,),
        in_specs=[pl.BlockSpec((bsz, D_MODEL), lambda j: (0, 0)),
                  pl.BlockSpec((D_MODEL, D_MODEL), lambda j: (0, j)),
                  pl.BlockSpec((1, D_MODEL), lambda j: (0, j))],
        out_specs=pl.BlockSpec((1, bsz, D_MODEL), lambda j: (j, 0, 0)),
        compiler_params=pltpu.CompilerParams(dimension_semantics=("arbitrary",)),
        name="adaln_mods",
    )(c, w_ada, b_ada2)


def _ffn_kernel(*refs, mod_row, tiles_per_seq, n_cast):
    refs = list(refs)
    h_ref, mods_ref, gn_ref, wg_ref, wu_ref, wd_ref = refs[:6]
    del refs[:6]
    cast_in = refs[:n_cast]
    del refs[:n_cast]
    o_ref = refs.pop(0)
    cast_out = refs[:n_cast]
    del refs[:n_cast]
    xn_ref, ha_ref, act_ref, sproj_ref = refs

    step = pl.program_id(0)
    c0 = slice(0, FF_CHUNK)
    sub = xn_ref.shape[0]

    @pl.when(step == 0)
    def _():
        sproj_ref[:, :FF_CHUNK] = _dot(mods_ref[mod_row], wg_ref[:, c0])
        sproj_ref[:, FF_CHUNK:] = _dot(mods_ref[mod_row], wu_ref[:, c0])

    _run_cast_riders(cast_in, cast_out)
    seq_row = pl.ds(step // tiles_per_seq, 1)
    shift = mods_ref[mod_row, seq_row, :]
    gate = mods_ref[mod_row + 2, seq_row, :]

    for r0 in range(0, h_ref.shape[0], sub):
        rows = slice(r0, r0 + sub)
        h = h_ref[rows, :]
        ha_ref[...] = h * (gn_ref[...] * (1.0 + mods_ref[mod_row + 1, seq_row, :]))
        r = lax.rsqrt(jnp.mean(h * h, axis=-1, keepdims=True) + EPS)
        ha = ha_ref[...]
        g = r * _dot(ha, wg_ref[:, c0]) + sproj_ref[seq_row, :FF_CHUNK]
        u = r * _dot(ha, wu_ref[:, c0]) + sproj_ref[seq_row, FF_CHUNK:]
        act_ref[:, c0] = jax.nn.silu(g) * u
        xn_ref[...] = ha_ref[...] * r + shift
        for c in range(1, D_FF // FF_CHUNK):
            cs = slice(c * FF_CHUNK, (c + 1) * FF_CHUNK)
            xn = xn_ref[...]
            g = _dot(xn, wg_ref[:, cs])
            u = _dot(xn, wu_ref[:, cs])
            act_ref[:, cs] = jax.nn.silu(g) * u
        ff = _dot(act_ref[...], wd_ref[...])
        o_ref[rows, :] = h_ref[rows, :] + 0.5 * gate * ff


def _ffn(h2d, mods, g_norm, w_gate, w_up, w_down, *, mod_row, seq, cast=(), in_place=False):
    n_tok = h2d.shape[0]
    n_mod_rows, bsz, _ = mods.shape
    tm = TM_FFN
    bf16_weights = w_gate.dtype == BF16
    sub = SUB_FFN_BF16_WEIGHTS if bf16_weights else SUB_FFN_F32_WEIGHTS
    vmem_limit = VMEM_LIMIT_FFN_BF16_WEIGHTS if bf16_weights else VMEM_LIMIT_FFN_F32_WEIGHTS
    assert seq % tm == 0 and tm % sub == 0
    n_steps = n_tok // tm
    tiles_per_seq = seq // tm
    row_spec = pl.BlockSpec((tm, D_MODEL), lambda i: (i, 0))
    cast_specs, cast_shapes = _cast_rider_specs(cast, n_steps)
    in_specs = [row_spec,
                _const_spec((n_mod_rows, bsz, D_MODEL)),
                _const_spec((1, D_MODEL)),
                _const_spec(w_gate.shape),
                _const_spec(w_up.shape),
                _const_spec(w_down.shape)] + cast_specs
    out_specs = [row_spec] + cast_specs
    out_shapes = [jax.ShapeDtypeStruct((n_tok, D_MODEL), F32)] + cast_shapes
    operands = [h2d, mods, g_norm, w_gate, w_up, w_down, *cast]
    return pl.pallas_call(
        functools.partial(_ffn_kernel, mod_row=mod_row, tiles_per_seq=tiles_per_seq,
                          n_cast=len(cast)),
        out_shape=out_shapes,
        grid=(n_steps,),
        in_specs=in_specs,
        out_specs=out_specs,
        scratch_shapes=[pltpu.VMEM((sub, D_MODEL), F32),
                        pltpu.VMEM((sub, D_MODEL), F32),
                        pltpu.VMEM((sub, D_FF), F32),
                        pltpu.VMEM((bsz, 2 * FF_CHUNK), F32)],
        compiler_params=pltpu.CompilerParams(
            dimension_semantics=("arbitrary",), vmem_limit_bytes=vmem_limit),
        input_output_aliases={0: 0} if in_place else {},
        name=f"ffn_mod{mod_row}",
    )(*operands)


def _head_rms(x, gain):
    lane = lax.broadcasted_iota(jnp.int32, (x.shape[0], LANES), 1)
    left = lane < HEAD_DIM
    cols = []
    for t in range(x.shape[1] // LANES):
        xt = x[:, t * LANES:(t + 1) * LANES]
        sq = xt * xt
        s_all = jnp.sum(sq, axis=-1, keepdims=True)
        s_left = jnp.sum(jnp.where(left, sq, 0.0), axis=-1, keepdims=True)
        ms = jnp.where(left, s_left, s_all - s_left) * (1.0 / HEAD_DIM)
        cols.append(xt * lax.rsqrt(ms + EPS))
    return jnp.concatenate(cols, axis=1) * gain


def _pair_blocks(x):
    lane = lax.broadcasted_iota(jnp.int32, x.shape, 1)
    left = lane < HEAD_DIM
    xr = pltpu.roll(x, HEAD_DIM, axis=1)
    blk0 = jnp.concatenate([jnp.where(left, x, 0.0), jnp.where(left, 0.0, xr)], axis=0)
    blk1 = jnp.concatenate([jnp.where(left, xr, 0.0), jnp.where(left, 0.0, x)], axis=0)
    return blk0.astype(BF16), blk1.astype(BF16)


def _mixer_kernel(*refs, mod_row, tiles_per_seq, n_cast):
    refs = list(refs)
    (sink_ref, h_ref, mods_ref, gn_ref, win_ref, gln_ref, bln_ref, ws_ref, bs_ref, gq_ref, gk_ref,
     wa_ref, wb_ref, wo_ref) = refs[:14]
    del refs[:14]
    cast_in = refs[:n_cast]
    del refs[:n_cast]
    o_ref = refs.pop(0)
    cast_out = refs[:n_cast]
    del refs[:n_cast]
    (xn_ref, uv_ref, vn_ref, ya_ref, qkv_ref, qn_ref, attn_ref, k_ref, v_ref,
     sg_ref, mg_ref, ha_ref, sproj_ref) = refs
    tm = h_ref.shape[0]
    first_tile = (pl.program_id(0) % tiles_per_seq) == 0

    @pl.when(first_tile)
    def _():
        k_ref[0:CHUNK, :] = jnp.zeros((CHUNK, D_KV), F32)
        v_ref[0:CHUNK, :] = jnp.zeros((CHUNK, D_KV), F32)

    qkv_cols = slice(COL_QKV, COL_QKV + D_B + 2 * D_KV)

    @pl.when(pl.program_id(0) == 0)
    def _():
        sproj_ref[...] = _dot(mods_ref[mod_row].astype(BF16), win_ref[:, qkv_cols])

    _run_cast_riders(cast_in, cast_out)
    seq_row = pl.ds(pl.program_id(0) // tiles_per_seq, 1)

    own_key = (lax.broadcasted_iota(jnp.int32, (CHUNK, CHUNK), 1)
               <= lax.broadcasted_iota(jnp.int32, (CHUNK, CHUNK), 0))
    prev_cap = jnp.where(first_tile, NEG, jnp.finfo(F32).max)
    lane = lax.broadcasted_iota(jnp.int32, (CHUNK, LANES), 1)
    left = lane < HEAD_DIM
    e_head = lax.broadcasted_iota(jnp.int32, (4 * CHUNK, LANES), 0) // (2 * CHUNK)
    e_lane = lax.broadcasted_iota(jnp.int32, (4 * CHUNK, LANES), 1) // HEAD_DIM
    ones_cols = jnp.where(e_head == e_lane, 1.0, 0.0).astype(BF16)
    tiles_per_kv = (N_Q_HEADS // N_KV_HEADS) // HEADS_PER_VREG

    def sub_tile(r0):
        rows = slice(r0, r0 + SUB_MIX)
        n_blk = SUB_MIX // CHUNK
        b0 = r0 // CHUNK
        blocks = [slice(r0 + b * CHUNK, r0 + (b + 1) * CHUNK) for b in range(n_blk)]

        h = h_ref[rows, :]
        ha = h * (gn_ref[...] * (1.0 + mods_ref[mod_row + 1, seq_row, :]))
        ha_ref[rows, :] = ha.astype(BF16)
        r = lax.rsqrt(jnp.mean(h * h, axis=-1, keepdims=True) + EPS)
        qkv_ref[rows, :] = r * jnp.dot(ha_ref[rows, :], win_ref[:, qkv_cols],
                                       preferred_element_type=F32) + sproj_ref[seq_row, :]
        xn_ref[rows, :] = (ha * r + mods_ref[mod_row, seq_row, :]).astype(BF16)

        n_gate_slices = 2 * n_blk
        gate_w = 2 * D_MODEL // n_gate_slices
        gate_slices = iter(range(n_gate_slices))

        def gate_slice():
            c0 = next(gate_slices) * gate_w
            sg_ref[rows, c0:c0 + gate_w] = jax.nn.sigmoid(jnp.dot(
                xn_ref[rows, :], win_ref[:, COL_GATES + c0:COL_GATES + c0 + gate_w],
                preferred_element_type=F32))

        uv_ref[rows, :] = jnp.dot(xn_ref[rows, :], win_ref[:, COL_UV:COL_UV + 2 * D_A],
                                  preferred_element_type=F32)

        def mixer_a_block(rs):
            v = jax.nn.gelu(uv_ref[rs, D_A:])
            mu = jnp.mean(v, axis=-1, keepdims=True)
            var = jnp.mean(jnp.square(v - mu), axis=-1, keepdims=True)
            vn_ref[rs, :] = ((v - mu) * lax.rsqrt(var + EPS) * gln_ref[...] + bln_ref[...]).astype(BF16)
            uv_ref[rs, :D_A] = jax.nn.gelu(uv_ref[rs, :D_A])

        def spatial_gate():
            t_idx = lax.broadcasted_iota(jnp.int32, (CHUNK, CHUNK), 0)
            s_idx = lax.broadcasted_iota(jnp.int32, (CHUNK, CHUNK), 1)
            causal = s_idx <= t_idx
            for g in range(SGU_GROUPS):
                gs = slice(g * SGU_GROUP_DIM, (g + 1) * SGU_GROUP_DIM)
                w_g = jnp.where(causal, ws_ref[g], 0.0).astype(BF16)
                v_cat = jnp.concatenate([vn_ref[rs, gs] for rs in blocks], axis=1)
                z = jnp.dot(w_g, v_cat, preferred_element_type=F32)
                for b, rs in enumerate(blocks):
                    zb = z[:, b * SGU_GROUP_DIM:(b + 1) * SGU_GROUP_DIM] + bs_ref[:, gs]
                    ya_ref[rs, gs] = (uv_ref[rs, gs] * zb).astype(BF16)

        for b, rs in enumerate(blocks):
            qn_ref[rs, :] = (_head_rms(qkv_ref[rs, :D_B], gq_ref[...])
                             * (HEAD_DIM ** -0.5)).astype(BF16)
            k_ref[(b0 + b + 1) * CHUNK:(b0 + b + 2) * CHUNK, :] = _head_rms(
                qkv_ref[rs, D_B:D_B + D_KV], gk_ref[...])
            v_ref[(b0 + b + 1) * CHUNK:(b0 + b + 2) * CHUNK, :] = qkv_ref[rs, D_B + D_KV:]

        def attention_block(b, rs):
            kblk = _pair_blocks(k_ref[(b0 + b) * CHUNK:(b0 + b + 2) * CHUNK, :])
            vblk = _pair_blocks(v_ref[(b0 + b) * CHUNK:(b0 + b + 2) * CHUNK, :])
            for j in range(N_KV_HEADS):
                tiles = [j * tiles_per_kv + t for t in range(tiles_per_kv)]
                q_rows = jnp.concatenate([qn_ref[rs, t * LANES:(t + 1) * LANES] for t in tiles], axis=0)
                s = lax.dot_general(q_rows, kblk[j], (((1,), (1,)), ((), ())),
                                    preferred_element_type=F32)
                probs, sink_terms = [], []
                for i, t in enumerate(tiles):
                    p_blocks, e_sinks = [], []
                    for hh in range(HEADS_PER_VREG):
                        sink = sink_ref[t * HEADS_PER_VREG + hh]
                        col = hh * 2 * CHUNK
                        s_prev = s[i * CHUNK:(i + 1) * CHUNK, col:col + CHUNK]
                        s_own = s[i * CHUNK:(i + 1) * CHUNK, col + CHUNK:col + 2 * CHUNK]
                        if b0 + b == 0:
                            s_prev = jnp.minimum(s_prev, prev_cap)
                        folded = jnp.where(own_key, s_own, s_prev)
                        m = jnp.maximum(jnp.max(folded, axis=-1, keepdims=True), sink)
                        e = jnp.exp(folded - m)
                        p_blocks.append(jnp.where(own_key, 0.0, e).astype(BF16))
                        p_blocks.append(jnp.where(own_key, e, 0.0).astype(BF16))
                        e_sinks.append(jnp.exp(sink - m))
                    probs.append(jnp.concatenate(p_blocks, axis=1))
                    sink_terms.append(jnp.where(left, e_sinks[0], e_sinks[1]))
                pv = jnp.dot(jnp.concatenate(probs, axis=0),
                             jnp.concatenate([vblk[j], ones_cols], axis=1),
                             preferred_element_type=F32)
                out = pv[:, :LANES] / (pv[:, LANES:] + jnp.concatenate(sink_terms, axis=0))
                for i, t in enumerate(tiles):
                    attn_ref[rs, t * LANES:(t + 1) * LANES] = out[i * CHUNK:(i + 1) * CHUNK].astype(BF16)

        for b, rs in enumerate(blocks):
            mixer_a_block(rs)
            gate_slice()
            attention_block(b, rs)
            gate_slice()
        spatial_gate()

        for c in range(D_MODEL // MXU_DIM):
            cs = slice(c * MXU_DIM, (c + 1) * MXU_DIM)
            cs_b = slice(D_MODEL + c * MXU_DIM, D_MODEL + (c + 1) * MXU_DIM)
            y_a = jnp.dot(ya_ref[rows, :], wa_ref[:, cs], preferred_element_type=F32)
            y_b = jnp.dot(attn_ref[rows, :], wb_ref[:, cs], preferred_element_type=F32)
            mg_ref[rows, cs] = (sg_ref[rows, cs] * y_a + sg_ref[rows, cs_b] * y_b).astype(BF16)
        out = jnp.dot(mg_ref[rows, :], wo_ref[...], preferred_element_type=F32)
        o_ref[rows, :] = h_ref[rows, :] + mods_ref[mod_row + 2, seq_row, :] * out

    for r0 in range(0, tm, SUB_MIX):
        sub_tile(r0)

    k_ref[0:CHUNK, :] = k_ref[tm:tm + CHUNK, :]
    v_ref[0:CHUNK, :] = v_ref[tm:tm + CHUNK, :]


def _mixer(h2d, mods, g_norm, w_in, g_ln, b_ln, w_spatial, bias_rows, g_q, g_k, sinks,
           w_a, w_b, w_out, *, mod_row, seq, cast=()):
    n_tok = h2d.shape[0]
    n_mod_rows, bsz, _ = mods.shape
    n_steps = n_tok // TM_MIX
    tiles_per_seq = seq // TM_MIX
    row_spec = pl.BlockSpec((TM_MIX, D_MODEL), lambda i: (i, 0))
    cast_specs, cast_shapes = _cast_rider_specs(cast, n_steps)
    return pl.pallas_call(
        functools.partial(_mixer_kernel, mod_row=mod_row, tiles_per_seq=tiles_per_seq,
                          n_cast=len(cast)),
        out_shape=[jax.ShapeDtypeStruct((n_tok, D_MODEL), F32)] + cast_shapes,
        grid=(n_steps,),
        in_specs=[pl.BlockSpec(memory_space=pltpu.SMEM),
                  row_spec,
                  _const_spec((n_mod_rows, bsz, D_MODEL)),
                  _const_spec((1, D_MODEL)),
                  _const_spec((D_MODEL, IN_COLS)),
                  _const_spec((1, D_A)),
                  _const_spec((1, D_A)),
                  _const_spec((SGU_GROUPS, CHUNK, CHUNK)),
                  _const_spec((CHUNK, D_A)),
                  _const_spec((1, D_B)),
                  _const_spec((1, D_KV)),
                  _const_spec((D_A, D_MODEL)),
                  _const_spec((D_B, D_MODEL)),
                  _const_spec((D_MODEL, D_MODEL))] + cast_specs,
        out_specs=[row_spec] + cast_specs,
        scratch_shapes=[pltpu.VMEM((TM_MIX, D_MODEL), BF16),
                        pltpu.VMEM((TM_MIX, 2 * D_A), F32),
                        pltpu.VMEM((TM_MIX, D_A), BF16),
                        pltpu.VMEM((TM_MIX, D_A), BF16),
                        pltpu.VMEM((TM_MIX, D_B + 2 * D_KV), F32),
                        pltpu.VMEM((TM_MIX, D_B), BF16),
                        pltpu.VMEM((TM_MIX, D_B), BF16),
                        pltpu.VMEM((TM_MIX + CHUNK, D_KV), F32),
                        pltpu.VMEM((TM_MIX + CHUNK, D_KV), F32),
                        pltpu.VMEM((TM_MIX, 2 * D_MODEL), F32),
                        pltpu.VMEM((TM_MIX, D_MODEL), BF16),
                        pltpu.VMEM((TM_MIX, D_MODEL), BF16),
                        pltpu.VMEM((bsz, D_B + 2 * D_KV), F32)],
        compiler_params=pltpu.CompilerParams(
            dimension_semantics=("arbitrary",), vmem_limit_bytes=VMEM_LIMIT_MIX),
        input_output_aliases={1: 0},
        name="mixer",
    )(sinks, h2d, mods, g_norm, w_in, g_ln, b_ln, w_spatial, bias_rows, g_q, g_k,
      w_a, w_b, w_out, *cast)


def kernel(x, c, w_ada, b_ada, g_norm1, ffn1_w_gate, ffn1_w_up, ffn1_w_down, g_norm2, w_in,
           g_sgu_ln, b_sgu_ln, w_spatial, b_spatial, g_q, g_k, attn_sinks, w_branch_a,
           w_branch_b, w_out, g_norm3, ffn2_w_gate, ffn2_w_up, ffn2_w_down):
    bsz, seq, d = x.shape
    assert d == D_MODEL and seq % TM_MIX == 0
    assert w_ada.shape[0] == 1, "single-layer problem"
    h = x.reshape(bsz * seq, d)
    b_ada2 = b_ada[0].reshape(1, N_MOD * d)

    mods = _adaln_mods(c, w_ada[0], b_ada2, N_MOD)

    h, w_in_b, w_a_b, w_b_b, w_out_b = _ffn(
        h, mods, g_norm1, ffn1_w_gate[0], ffn1_w_up[0], ffn1_w_down[0], mod_row=0, seq=seq,
        cast=(w_in[0], w_branch_a[0], w_branch_b[0], w_out[0]))

    bias_rows = jnp.repeat(b_spatial[0].T, SGU_GROUP_DIM, axis=1)
    h, w_gate_b, w_up_b, w_down_b = _mixer(
        h, mods, g_norm2, w_in_b, g_sgu_ln, b_sgu_ln, w_spatial[0], bias_rows,
        jnp.tile(g_q[0], N_Q_HEADS)[None], jnp.tile(g_k[0], N_KV_HEADS)[None], attn_sinks[0],
        w_a_b, w_b_b, w_out_b, mod_row=3, seq=seq,
        cast=(ffn2_w_gate[0], ffn2_w_up[0], ffn2_w_down[0]))

    (h,) = _ffn(h, mods, g_norm3, w_gate_b, w_up_b, w_down_b, mod_row=6, seq=seq, in_place=True)
    return h.reshape(bsz, seq, d)
```
